```python
import jax, jax.numpy as jnp
from jax import lax
import numpy as np

D_MODEL = 1024
BATCH = 16
SEQ = 256
DEPTH = 4
DEC_BATCH = 2
DEC_SEQ = 4096
PAST_LEN = 256

GRID_W = 64
N_MIXERS = 2
N_A_LAYERS = (DEPTH + 1) // 2
N_B_LAYERS = DEPTH // 2
CHUNK = 128
A_WIDTH = D_MODEL
A_GROUPS = 8
A_GROUP_DIM = A_WIDTH // A_GROUPS
NA_HEADS = 16
NA_HEAD_DIM = D_MODEL // NA_HEADS
NA_MAX_ROWS = 8
NA_COLS = 16
Q_BLOCK = 128
PEER_HEADS = 8
PEER_KEY_DIM = 256
PEER_HALF = PEER_KEY_DIM // 2
PEER_N_KEYS = 128
PEER_N_EXPERTS = PEER_N_KEYS * PEER_N_KEYS
PEER_TOPK = 16
PEER_TOKEN_BLOCK = 128
RMS_EPS = 1e-6

kernel_name = "hybrid_chunkmlp_natten_peer_diffusion_step"


def rmsnorm(x, w):
    xf = x.astype(jnp.float32)
    y = xf * lax.rsqrt(jnp.mean(xf * xf, axis=-1, keepdims=True) + RMS_EPS)
    return (y * w.astype(jnp.float32)).astype(x.dtype)


def ada_mod(cond, w, b):
    m = jnp.dot(jax.nn.silu(cond), w) + b
    return jnp.split(m, 6, axis=-1)


def chunk_mlp_mixer(h, w_in, w_s, b_s, v_norm_w, w_out):
    bsz, L, _ = h.shape
    z = jax.nn.gelu(h @ w_in)
    u, v = jnp.split(z, 2, axis=-1)
    v = rmsnorm(v, v_norm_w).reshape(bsz, L // CHUNK, CHUNK, A_GROUPS, A_GROUP_DIM)
    v = jnp.einsum('gpq,bnqgc->bnpgc', w_s, v) + b_s.T[None, None, :, :, None]
    return (u * v.reshape(bsz, L, A_WIDTH)) @ w_out


def qkv_heads(h, w_qkv, q_norm_w, k_norm_w):
    bsz, L, _ = h.shape
    qkv = (h @ w_qkv).reshape(bsz, L, 3, NA_HEADS, NA_HEAD_DIM)
    q = rmsnorm(qkv[:, :, 0], q_norm_w)
    k = rmsnorm(qkv[:, :, 1], k_norm_w)
    return q, k, qkv[:, :, 2]


def context_attention(q, k, v):
    bsz, L = q.shape[:2]
    scale = NA_HEAD_DIM ** -0.5
    qb = jnp.moveaxis(q.reshape(bsz, L // Q_BLOCK, Q_BLOCK, NA_HEADS, NA_HEAD_DIM), 1, 0)

    def block(qi):
        s = jnp.einsum('bqhd,bkhd->bhqk', qi, k).astype(jnp.float32) * scale
        p = jax.nn.softmax(s, axis=-1).astype(v.dtype)
        return jnp.einsum('bhqk,bkhd->bqhd', p, v)

    o = lax.map(block, qb)
    return jnp.moveaxis(o, 0, 1).reshape(bsz, L, NA_HEADS * NA_HEAD_DIM)


def neighbourhood_attention(q, k, v, k_ctx, v_ctx, rpb):
    bsz, L = q.shape[:2]
    rows = L // GRID_W
    wr = min(NA_MAX_ROWS, rows)
    n_loc = wr * NA_COLS
    scale = NA_HEAD_DIM ** -0.5
    grid = (bsz, rows, GRID_W, NA_HEADS, NA_HEAD_DIM)
    qg, kg, vg = q.reshape(grid), k.reshape(grid), v.reshape(grid)
    cols = np.arange(GRID_W)
    col_start = np.clip(cols - NA_COLS // 2, 0, GRID_W - NA_COLS)
    col_idx = col_start[:, None] + np.arange(NA_COLS)[None, :]
    dc_idx = jnp.asarray(col_idx - cols[:, None] + NA_COLS - 1)

    def row_block(r):
        r0 = jnp.clip(r - wr // 2, 0, rows - wr)
        q_r = lax.dynamic_index_in_dim(qg, r, axis=1, keepdims=False)
        k_win = lax.dynamic_slice_in_dim(kg, r0, wr, axis=1)[:, :, col_idx]
        v_win = lax.dynamic_slice_in_dim(vg, r0, wr, axis=1)[:, :, col_idx]
        dr_idx = r0 + jnp.arange(wr) - r + NA_MAX_ROWS - 1
        bias = rpb[:, dr_idx[None, :, None], dc_idx[:, None, :]]
        s_loc = jnp.einsum('bqhd,brqjhd->bhqrj', q_r, k_win).astype(jnp.float32) * scale
        s_loc = s_loc + bias.astype(jnp.float32)[None]
        s_ctx = jnp.einsum('bqhd,bkhd->bhqk', q_r, k_ctx).astype(jnp.float32) * scale
        s = jnp.concatenate([s_loc.reshape(bsz, NA_HEADS, GRID_W, n_loc), s_ctx], axis=-1)
        p = jax.nn.softmax(s, axis=-1).astype(v.dtype)
        p_loc = p[..., :n_loc].reshape(bsz, NA_HEADS, GRID_W, wr, NA_COLS)
        return (jnp.einsum('bhqrj,brqjhd->bqhd', p_loc, v_win)
                + jnp.einsum('bhqk,bkhd->bqhd', p[..., n_loc:], v_ctx))

    o = lax.map(row_block, jnp.arange(rows))
    return jnp.moveaxis(o, 0, 1).reshape(bsz, L, NA_HEADS * NA_HEAD_DIM)


def peer_ffn(h, w_q, sub_keys, u_tab, v_tab):
    shp = h.shape
    hf = h.reshape(-1, D_MODEL)
    T = hf.shape[0]
    q = (hf @ w_q).reshape(T, PEER_HEADS, 2, PEER_HALF)
    s = jnp.einsum('thpd,pkd->thpk', q, sub_keys).astype(jnp.float32)
    s1, i1 = lax.top_k(s[:, :, 0], PEER_TOPK)
    s2, i2 = lax.top_k(s[:, :, 1], PEER_TOPK)
    cand = (s1[..., :, None] + s2[..., None, :]).reshape(T, PEER_HEADS, PEER_TOPK * PEER_TOPK)
    sc, ci = lax.top_k(cand, PEER_TOPK)
    e1 = jnp.take_along_axis(i1, ci // PEER_TOPK, axis=-1)
    e2 = jnp.take_along_axis(i2, ci % PEER_TOPK, axis=-1)
    idx = e1 * PEER_N_KEYS + e2
    g = jax.nn.softmax(sc, axis=-1).astype(h.dtype)
    nb = T // PEER_TOKEN_BLOCK

    def block(args):
        hb, ib, gb = args
        a = jax.nn.gelu(jnp.einsum('td,thkd->thk', hb, u_tab[ib]))
        return jnp.einsum('thk,thkd->td', gb * a, v_tab[ib])

    out = lax.map(block, (hf.reshape(nb, PEER_TOKEN_BLOCK, D_MODEL),
                          idx.reshape(nb, PEER_TOKEN_BLOCK, PEER_HEADS, PEER_TOPK),
                          g.reshape(nb, PEER_TOKEN_BLOCK, PEER_HEADS, PEER_TOPK)))
    return out.reshape(shp)


def setup_inputs(seed: int = 0) -> dict:
    key = jax.random.key(seed)
    ks = jax.random.split(key, 24)
    nrm = jax.random.normal
    f32 = jnp.float32
    D = D_MODEL
    return {
        "x_prompt": nrm(ks[0], (BATCH, SEQ, D), f32),
        "x_sample": nrm(ks[1], (DEC_BATCH, DEC_SEQ, D), f32),
        "cache_k": nrm(ks[2], (DEC_BATCH, N_B_LAYERS, PAST_LEN, NA_HEADS, NA_HEAD_DIM), f32),
        "cache_v": nrm(ks[3], (DEC_BATCH, N_B_LAYERS, PAST_LEN, NA_HEADS, NA_HEAD_DIM), f32),
        "c": nrm(ks[4], (DEC_BATCH, D), f32),
        "c_ctx": nrm(ks[5], (D,), f32),
        "ada_w": nrm(ks[6], (DEPTH, D, 6 * D), f32) * (0.5 * D ** -0.5),
        "ada_b": nrm(ks[7], (DEPTH, 6 * D), f32) * 0.02,
        "norm1_w": 1.0 + 0.02 * nrm(ks[8], (DEPTH, D), f32),
        "norm2_w": 1.0 + 0.02 * nrm(ks[9], (DEPTH, D), f32),
        "a_w_in": nrm(ks[10], (N_A_LAYERS, D, 2 * A_WIDTH), f32) * D ** -0.5,
        "a_w_s": nrm(ks[11], (N_A_LAYERS, A_GROUPS, CHUNK, CHUNK), f32) * CHUNK ** -0.5,
        "a_b_s": nrm(ks[12], (N_A_LAYERS, A_GROUPS, CHUNK), f32) * 0.02,
        "a_v_norm": 1.0 + 0.02 * nrm(ks[13], (N_A_LAYERS, A_WIDTH), f32),
        "a_w_out": nrm(ks[14], (N_A_LAYERS, A_WIDTH, D), f32) * A_WIDTH ** -0.5,
        "b_w_qkv": nrm(ks[15], (N_B_LAYERS, D, 3 * D), f32) * D ** -0.5,
        "b_q_norm": 1.0 + 0.02 * nrm(ks[16], (N_B_LAYERS, NA_HEAD_DIM), f32),
        "b_k_norm": 1.0 + 0.02 * nrm(ks[17], (N_B_LAYERS, NA_HEAD_DIM), f32),
        "b_rpb": nrm(ks[18], (N_B_LAYERS, NA_HEADS, 2 * NA_MAX_ROWS - 1, 2 * NA_COLS - 1), f32) * 0.1,
        "b_w_out": nrm(ks[19], (N_B_LAYERS, D, D), f32) * D ** -0.5,
        "peer_w_q": nrm(ks[20], (DEPTH, D, PEER_HEADS * PEER_KEY_DIM), f32) * D ** -0.5,
        "peer_sub_keys": nrm(ks[21], (DEPTH, 2, PEER_N_KEYS, PEER_HALF), f32) * PEER_HALF ** -0.5,
        "peer_u": nrm(ks[22], (DEPTH, PEER_N_EXPERTS, D), f32) * D ** -0.5,
        "peer_v": nrm(ks[23], (DEPTH, PEER_N_EXPERTS, D), f32) * PEER_HEADS ** -0.5,
    }


def reference(x_prompt, x_sample, cache_k, cache_v, c, c_ctx, ada_w, ada_b, norm1_w, norm2_w,
              a_w_in, a_w_s, a_b_s, a_v_norm, a_w_out, b_w_qkv, b_q_norm, b_k_norm, b_rpb,
              b_w_out, peer_w_q, peer_sub_keys, peer_u, peer_v):
    x = x_prompt
    new_k, new_v = [], []
    for i in range(DEPTH):
        j = i // N_MIXERS
        sh1, sc1, g1, sh2, sc2, g2 = ada_mod(c_ctx, ada_w[i], ada_b[i])
        h = rmsnorm(x, norm1_w[i]) * (1 + sc1) + sh1
        if i % N_MIXERS == 0:
            mix = chunk_mlp_mixer(h, a_w_in[j], a_w_s[j], a_b_s[j], a_v_norm[j], a_w_out[j])
        else:
            q, k, v = qkv_heads(h, b_w_qkv[j], b_q_norm[j], b_k_norm[j])
            new_k.append(k)
            new_v.append(v)
            mix = context_attention(q, k, v) @ b_w_out[j]
        x = x + g1 * mix
        h = rmsnorm(x, norm2_w[i]) * (1 + sc2) + sh2
        x = x + g2 * peer_ffn(h, peer_w_q[i], peer_sub_keys[i], peer_u[i], peer_v[i])
    y_prompt = x
    new_cache_k = jnp.stack(new_k, axis=1)
    new_cache_v = jnp.stack(new_v, axis=1)

    x = x_sample
    cond = c[:, None, :]
    for i in range(DEPTH):
        j = i // N_MIXERS
        sh1, sc1, g1, sh2, sc2, g2 = ada_mod(cond, ada_w[i], ada_b[i])
        h = rmsnorm(x, norm1_w[i]) * (1 + sc1) + sh1
        if i % N_MIXERS == 0:
            mix = chunk_mlp_mixer(h, a_w_in[j], a_w_s[j], a_b_s[j], a_v_norm[j], a_w_out[j])
        else:
            q, k, v = qkv_heads(h, b_w_qkv[j], b_q_norm[j], b_k_norm[j])
            mix = neighbourhood_attention(q, k, v, cache_k[:, j], cache_v[:, j], b_rpb[j]) @ b_w_out[j]
        x = x + g1 * mix
        h = rmsnorm(x, norm2_w[i]) * (1 + sc2) + sh2
        x = x + g2 * peer_ffn(h, peer_w_q[i], peer_sub_keys[i], peer_u[i], peer_v[i])
    y_sample = x
    return (y_prompt, y_sample, new_cache_k, new_cache_v)
```

```python
import functools

import jax
import jax.numpy as jnp
from jax import lax
from jax.experimental import pallas as pl
from jax.experimental.pallas import tpu as pltpu

F32 = jnp.float32
BF16 = jnp.bfloat16

D = 1024
DEPTH = 4
SEG = 4096
N_SEG = 3
T = N_SEG * SEG
SEQ = 256
GRID_W = 64
CHUNK = 128
A_GROUPS = 8
HEADS = 16
HEAD_DIM = 64
NA_ROWS = 8
NA_COLS = 16
P_HEADS = 8
P_HALF = 128
P_KEYS = 128
P_TOPK = 16
P_PAIRS = P_HEADS * P_TOPK
N_EXPERTS = P_KEYS * P_KEYS
EPS = 1e-6
NEG = -1e30

VMEM_LIMIT = 56 * 1024 * 1024

TM = 256
PEER_TB = 64
PEER_G = 8
PEER_ROWS = PEER_G * P_PAIRS


def _params(*sem):
    return pltpu.CompilerParams(dimension_semantics=sem, vmem_limit_bytes=VMEM_LIMIT)


def _dot(a, b):
    return jnp.dot(a, b, preferred_element_type=F32)


def _dot_nt(a, b):
    return lax.dot_general(a, b, (((1,), (1,)), ((), ())), preferred_element_type=F32)


def _split(a):
    hi = a.astype(BF16)
    lo = (a - hi.astype(F32)).astype(BF16)
    return hi, lo


def _dot_f32(a, b):
    ah, al = _split(a)
    bh, bl = _split(b)
    return _dot(ah, bh) + _dot(ah, bl) + _dot(al, bh)


def _dot_f32_exact_rhs(a, b_bf16):
    ah, al = _split(a)
    return _dot(ah, b_bf16) + _dot(al, b_bf16)


def _rms_mod(x, norm_w, scale, shift):
    y = x * lax.rsqrt(jnp.mean(x * x, axis=-1, keepdims=True) + EPS)
    return (y * norm_w) * (1.0 + scale) + shift


def _seg_of_block(tm):
    return lambda i: (i * tm) // SEG


def _ada_kernel(cond_ref, w_ref, b_ref, o_ref):
    c = cond_ref[...]
    s = c * jax.nn.sigmoid(c)
    o_ref[...] = _dot_f32(s, w_ref[...]) + b_ref[...]


def _ada_mod(cond8, ada_w, ada_b):
    tn = 1024
    n = 6 * D
    out = pl.pallas_call(
        _ada_kernel,
        grid=(DEPTH, n // tn),
        in_specs=[
            pl.BlockSpec((8, D), lambda l, j: (0, 0)),
            pl.BlockSpec((None, D, tn), lambda l, j: (l, 0, j)),
            pl.BlockSpec((None, 1, tn), lambda l, j: (l, 0, j)),
        ],
        out_specs=pl.BlockSpec((None, 8, tn), lambda l, j: (l, 0, j)),
        out_shape=jax.ShapeDtypeStruct((DEPTH, 8, n), F32),
        compiler_params=_params("parallel", "parallel"),
        name="ada_mod",
    )(cond8, ada_w, ada_b.reshape(DEPTH, 1, n))
    return out.reshape(DEPTH, 8, 6, D)


def _mix_a_kernel(x_ref, mod_ref, n1_ref, win_ref, ws_ref, bexp_ref, vn_ref, wout_ref, o_ref):
    x = x_ref[...]
    h = _rms_mod(x, n1_ref[...], mod_ref[1:2, :], mod_ref[0:1, :])
    z = jax.nn.gelu(_dot(h.astype(BF16), win_ref[...]))
    u = z[:, :D]
    v = z[:, D:]
    v = v * lax.rsqrt(jnp.mean(v * v, axis=-1, keepdims=True) + EPS) * vn_ref[...]
    vb = v.astype(BF16)
    rows = []
    for c in range(TM // CHUNK):
        cols = []
        for g in range(A_GROUPS):
            vg = vb[c * CHUNK:(c + 1) * CHUNK, g * 128:(g + 1) * 128]
            cols.append(_dot(ws_ref[g], vg))
        rows.append(jnp.concatenate(cols, axis=1) + bexp_ref[...])
    sv = jnp.concatenate(rows, axis=0)
    mix = _dot((u * sv).astype(BF16), wout_ref[...])
    o_ref[...] = x + mod_ref[2:3, :] * mix


def _mix_a(x, mod_l, n1, w_in, w_s, bexp, vn, w_out):
    const2 = lambda i: (0, 0)
    return pl.pallas_call(
        _mix_a_kernel,
        grid=(T // TM,),
        in_specs=[
            pl.BlockSpec((TM, D), lambda i: (i, 0)),
            pl.BlockSpec((None, 6, D), lambda i: ((i * TM) // SEG, 0, 0)),
            pl.BlockSpec((1, D), const2),
            pl.BlockSpec((D, 2 * D), const2),
            pl.BlockSpec((A_GROUPS, CHUNK, CHUNK), lambda i: (0, 0, 0)),
            pl.BlockSpec((CHUNK, D), const2),
            pl.BlockSpec((1, D), const2),
            pl.BlockSpec((D, D), const2),
        ],
        out_specs=pl.BlockSpec((TM, D), lambda i: (i, 0)),
        out_shape=jax.ShapeDtypeStruct((T, D), F32),
        compiler_params=_params("parallel"),
        name="mix_a",
    )(x, mod_l, n1, w_in, w_s, bexp, vn, w_out)


def _qkv_kernel(x_ref, mod_ref, n1_ref, w_ref, qn_ref, kn_ref, ind_ref, indt_ref,
                q_o, k_o, v_o, kf_o, vf_o):
    x = x_ref[...]
    h = _rms_mod(x, n1_ref[...], mod_ref[1:2, :], mod_ref[0:1, :])
    qkv = _dot(h.astype(BF16), w_ref[...])

    def head_norm(t, w):
        ss = _dot_f32_exact_rhs(t * t, ind_ref[...])
        inv = lax.rsqrt(ss * (1.0 / HEAD_DIM) + EPS)
        return t * _dot_f32_exact_rhs(inv, indt_ref[...]) * w

    q = head_norm(qkv[:, :D], qn_ref[...])
    k = head_norm(qkv[:, D:2 * D], kn_ref[...])
    v = qkv[:, 2 * D:]
    q_o[...] = (q * (HEAD_DIM ** -0.5)).astype(BF16)
    k_o[...] = k.astype(BF16)
    v_o[...] = v.astype(BF16)
    kf_o[...] = k
    vf_o[...] = v


def _qkv(x, mod_l, n1, w_qkv, qn, kn, ind, indt):
    const2 = lambda i: (0, 0)
    blk = pl.BlockSpec((TM, D), lambda i: (i, 0))
    return pl.pallas_call(
        _qkv_kernel,
        grid=(T // TM,),
        in_specs=[
            blk,
            pl.BlockSpec((None, 6, D), lambda i: ((i * TM) // SEG, 0, 0)),
            pl.BlockSpec((1, D), const2),
            pl.BlockSpec((D, 3 * D), const2),
            pl.BlockSpec((1, D), const2),
            pl.BlockSpec((1, D), const2),
            pl.BlockSpec((D, 128), const2),
            pl.BlockSpec((128, D), const2),
        ],
        out_specs=[blk, blk, blk, blk, blk],
        out_shape=[jax.ShapeDtypeStruct((T, D), BF16)] * 3 + [jax.ShapeDtypeStruct((T, D), F32)] * 2,
        compiler_params=_params("parallel"),
        name="qkv",
    )(x, mod_l, n1, w_qkv, qn, kn, ind, indt)


def _head_pair_attention(q2, score_parts):
    lane = lax.broadcasted_iota(jnp.int32, (1, 128), 1)
    lo = lane < HEAD_DIM
    out = None
    for sub in range(2):
        keep = lo if sub == 0 else jnp.logical_not(lo)
        qm = jnp.where(keep, q2, jnp.zeros_like(q2))
        scores = []
        for k2, _, bias_fn in score_parts:
            s = _dot_nt(qm, k2)
            b = bias_fn(sub)
            scores.append(s if b is None else s + b)
        m = scores[0].max(axis=-1, keepdims=True)
        for s in scores[1:]:
            m = jnp.maximum(m, s.max(axis=-1, keepdims=True))
        den = None
        acc = None
        for s, (_, v2, _) in zip(scores, score_parts):
            p = jnp.exp(s - m)
            d = p.sum(axis=-1, keepdims=True)
            o = _dot(p.astype(BF16), v2)
            den = d if den is None else den + d
            acc = o if acc is None else acc + o
        o = acc / den
        out = o if sub == 0 else jnp.where(lo, out, o)
    return out


def _ctx_attn_kernel(q_ref, k_ref, v_ref, x_ref, mod_ref, wout_ref, o_ref):
    outs = []
    for hp in range(HEADS // 2):
        sl = slice(hp * 128, (hp + 1) * 128)
        outs.append(_head_pair_attention(
            q_ref[:, sl], [(k_ref[:, sl], v_ref[:, sl], lambda sub: None)]))
    o = jnp.concatenate(outs, axis=1).astype(BF16)
    o_ref[...] = x_ref[...] + mod_ref[2:3, :] * _dot(o, wout_ref[...])


def _ctx_attn(q, k, v, x, mod_l, w_out):
    n = SEG // SEQ
    blk = pl.BlockSpec((SEQ, D), lambda b: (b, 0))
    return pl.pallas_call(
        _ctx_attn_kernel,
        grid=(n,),
        in_specs=[blk, blk, blk, blk,
                  pl.BlockSpec((None, 6, D), lambda b: (0, 0, 0)),
                  pl.BlockSpec((D, D), lambda b: (0, 0))],
        out_specs=blk,
        out_shape=jax.ShapeDtypeStruct((SEG, D), F32),
        compiler_params=_params("parallel"),
        name="ctx_attn",
    )(q, k, v, x, mod_l, w_out)


def _na_window_start(r):
    rows = SEG // GRID_W
    return jnp.clip(r - NA_ROWS // 2, 0, rows - NA_ROWS)


def _na_attn_kernel(q_ref, k_ref, v_ref, kc_ref, vc_ref, bias_ref, x_ref, mod_ref, wout_ref, o_ref):
    r = pl.program_id(1)
    start = pl.multiple_of(_na_window_start(r) * GRID_W, GRID_W)
    n_loc = NA_ROWS * GRID_W
    outs = []
    for hp in range(HEADS // 2):
        sl = slice(hp * 128, (hp + 1) * 128)
        kw = k_ref[pl.ds(start, n_loc), sl]
        vw = v_ref[pl.ds(start, n_loc), sl]
        outs.append(_head_pair_attention(
            q_ref[:, sl],
            [(kw, vw, lambda sub, hp=hp: bias_ref[2 * hp + sub]),
             (kc_ref[:, sl], vc_ref[:, sl], lambda sub: None)]))
    o = jnp.concatenate(outs, axis=1).astype(BF16)
    o_ref[...] = x_ref[...] + mod_ref[2:3, :] * _dot(o, wout_ref[...])


def _na_attn(q, k, v, kc, vc, bias, x, mod_l, w_out):
    rows = SEG // GRID_W
    nb = (T - SEG) // SEG
    qblk = pl.BlockSpec((GRID_W, D), lambda b, r: (rows + b * rows + r, 0))
    img = pl.BlockSpec((SEG, D), lambda b, r: (1 + b, 0))
    ctx = pl.BlockSpec((None, SEQ, D), lambda b, r: (b, 0, 0))

    def bias_map(b, r):
        return (_na_window_start(r) - r + NA_ROWS - 1, 0, 0, 0)

    return pl.pallas_call(
        _na_attn_kernel,
        grid=(nb, rows),
        in_specs=[qblk, img, img, ctx, ctx,
                  pl.BlockSpec((None, HEADS, GRID_W, NA_ROWS * GRID_W), bias_map),
                  qblk,
                  pl.BlockSpec((None, 6, D), lambda b, r: (1 + b, 0, 0)),
                  pl.BlockSpec((D, D), lambda b, r: (0, 0))],
        out_specs=pl.BlockSpec((GRID_W, D), lambda b, r: (b * rows + r, 0)),
        out_shape=jax.ShapeDtypeStruct((T - SEG, D), F32),
        compiler_params=_params("parallel", "arbitrary"),
        name="na_attn",
    )(q, k, v, kc, vc, bias, x, mod_l, w_out)


def _na_bias_table(rpb):
    qc = jnp.arange(GRID_W)
    c0 = jnp.clip(qc - NA_COLS // 2, 0, GRID_W - NA_COLS)
    kc = jnp.arange(GRID_W)
    inside = (kc[None, :] >= c0[:, None]) & (kc[None, :] < c0[:, None] + NA_COLS)
    dc = jnp.clip(kc[None, :] - qc[:, None] + NA_COLS - 1, 0, 2 * NA_COLS - 2)
    var = jnp.arange(NA_ROWS)
    dr = var[:, None] + jnp.arange(NA_ROWS)[None, :]
    tab = rpb[:, dr[:, :, None, None], dc[None, None, :, :]]
    tab = jnp.where(inside[None, None, None], tab, NEG)
    tab = jnp.transpose(tab, (1, 0, 3, 2, 4))
    return tab.reshape(NA_ROWS, HEADS, GRID_W, NA_ROWS * GRID_W).astype(F32)


def _peer_select_kernel(x_ref, mod_ref, n2_ref, wq_ref, sk_ref, seg_ref,
                        h_o, idx_o, g_o, q_scr):
    x = x_ref[...]
    h = _rms_mod(x, n2_ref[...], mod_ref[4:5, :], mod_ref[3:4, :])
    h_o[...] = h
    q_scr[...] = _dot(h.astype(BF16), wq_ref[...])

    n_c = P_TOPK * P_TOPK
    lane_k = lax.broadcasted_iota(jnp.int32, (TM, P_KEYS), 1)
    lane_c = lax.broadcasted_iota(jnp.int32, (TM, n_c), 1)
    lane_kf = lane_k.astype(F32)
    lane_cf = lane_c.astype(F32)
    lane_hi = lane_c // P_TOPK
    lane_lo = lane_c % P_TOPK
    neg_inf = jnp.float32(-jnp.inf)

    def head(hd, carry):
        idx_acc, sc_acc, m0_acc = carry
        base = hd * 2 * P_HALF
        cand = []
        cidx = []
        for p in range(2):
            off = pl.multiple_of(base + p * P_HALF, P_HALF)
            qhp = q_scr[:, pl.ds(off, P_HALF)].astype(BF16)
            s = _dot_nt(qhp, sk_ref[p])
            pos = lane_hi if p == 0 else lane_lo
            cv = jnp.zeros((TM, n_c), F32)
            ci = jnp.zeros((TM, n_c), F32)
            for it in range(P_TOPK):
                m = s.max(axis=-1, keepdims=True)
                am = jnp.where(s == m, lane_kf, float(P_KEYS)).min(axis=-1, keepdims=True)
                s = jnp.where(lane_kf == am, neg_inf, s)
                here = pos == it
                cv = jnp.where(here, m, cv)
                ci = jnp.where(here, am, ci)
            cand.append(cv)
            cidx.append(ci)
        cs = cand[0] + cand[1]
        ce = cidx[0] * float(P_KEYS) + cidx[1]
        for it in range(P_TOPK):
            m = cs.max(axis=-1, keepdims=True)
            am = jnp.where(cs == m, lane_cf, float(n_c)).min(axis=-1, keepdims=True)
            hit = lane_cf == am
            e = jnp.where(hit, ce, -1.0).max(axis=-1, keepdims=True)
            cs = jnp.where(hit, neg_inf, cs)
            slot = hd * P_TOPK + it
            idx_acc = jnp.where(lane_k == slot, e, idx_acc)
            sc_acc = jnp.where(lane_c == 2 * slot, m, sc_acc)
            if it == 0:
                m0_acc = jnp.where(lane_c // (2 * P_TOPK) == hd, m, m0_acc)
        return idx_acc, sc_acc, m0_acc

    idx_acc, sc_acc, m0_acc = lax.fori_loop(
        0, P_HEADS, head,
        (jnp.zeros((TM, P_PAIRS), F32), jnp.zeros((TM, 2 * P_PAIRS), F32),
         jnp.zeros((TM, 2 * P_PAIRS), F32)))
    even = (lane_c % 2) == 0
    ex = jnp.where(even, jnp.exp(sc_acc - m0_acc), 0.0)
    den = _dot_f32_exact_rhs(ex, seg_ref[...])
    idx_o[...] = idx_acc.astype(jnp.int32)
    g_o[...] = ex / den


def _peer_select(x, mod_l, n2, wq, sk, seg_ones):
    const2 = lambda i: (0, 0)
    blk = pl.BlockSpec((TM, D), lambda i: (i, 0))
    return pl.pallas_call(
        _peer_select_kernel,
        grid=(T // TM,),
        in_specs=[
            blk,
            pl.BlockSpec((None, 6, D), lambda i: ((i * TM) // SEG, 0, 0)),
            pl.BlockSpec((1, D), const2),
            pl.BlockSpec((D, P_HEADS * 2 * P_HALF), const2),
            pl.BlockSpec((2, P_KEYS, P_HALF), lambda i: (0, 0, 0)),
            pl.BlockSpec((2 * P_PAIRS, 2 * P_PAIRS), const2),
        ],
        out_specs=[blk,
                   pl.BlockSpec((TM, P_PAIRS), lambda i: (i, 0)),
                   pl.BlockSpec((TM, 2 * P_PAIRS), lambda i: (i, 0))],
        out_shape=[jax.ShapeDtypeStruct((T, D), F32),
                   jax.ShapeDtypeStruct((T, P_PAIRS), jnp.int32),
                   jax.ShapeDtypeStruct((T, 2 * P_PAIRS), F32)],
        scratch_shapes=[pltpu.VMEM((TM, P_HEADS * 2 * P_HALF), F32)],
        compiler_params=_params("parallel"),
        name="peer_select",
    )(x, mod_l, n2, wq, sk, seg_ones)


def _peer_expert_kernel(idx_ref, h_ref, g_ref, x_ref, mod_ref, tab_ref, o_ref, buf, sem):
    n_batches = PEER_TB // PEER_G
    half = D // 2

    def gather_wait(slot):
        pltpu.make_async_copy(tab_ref.at[pl.ds(0, PEER_ROWS), :], buf.at[slot], sem.at[slot]).wait()

    def gather_start(b, slot):
        def token(j, carry):
            t = b * PEER_G + j
            for k in range(P_PAIRS):
                e = idx_ref[t, k]
                pltpu.make_async_copy(
                    tab_ref.at[pl.ds(e, 1), :],
                    buf.at[slot, pl.ds(j * P_PAIRS + k, 1), :],
                    sem.at[slot]).start()
            return carry
        lax.fori_loop(0, PEER_G, token, 0)

    row = lax.broadcasted_iota(jnp.int32, (PEER_G, 2 * PEER_ROWS), 0)
    lane = lax.broadcasted_iota(jnp.int32, (PEER_G, 2 * PEER_ROWS), 1)
    own = (lane // (2 * P_PAIRS) == row) & (lane % 2 == 0)

    gather_start(0, 0)

    def batch(b, carry):
        slot = b % 2

        @pl.when(b + 1 < n_batches)
        def _():
            gather_start(b + 1, 1 - slot)

        gather_wait(slot)
        r0 = pl.multiple_of(b * PEER_G, PEER_G)
        h = h_ref[pl.ds(r0, PEER_G), :].astype(BF16)
        hx = jnp.concatenate([h[:, :half], h[:, half:]], axis=0)
        rows = pltpu.bitcast(buf[slot], BF16)
        dots = _dot_nt(hx, rows[:, :half])
        a = dots[:PEER_G] + pltpu.roll(dots[PEER_G:], 2 * PEER_ROWS - 1, 1)
        gate = jnp.concatenate([g_ref[pl.ds(r0, PEER_G), :]] * PEER_G, axis=1)
        w = jnp.where(own, jax.nn.gelu(a) * gate, 0.0)
        wx = jnp.concatenate([w, pltpu.roll(w, 1, 1)], axis=0).astype(BF16)
        y = _dot(wx, rows[:, half:])
        y = jnp.concatenate([y[:PEER_G], y[PEER_G:]], axis=1)
        o_ref[pl.ds(r0, PEER_G), :] = x_ref[pl.ds(r0, PEER_G), :] + mod_ref[5:6, :] * y
        return carry

    lax.fori_loop(0, n_batches, batch, 0)


def _peer_expert(idx, h, g, x, mod_l, table):
    blk = pl.BlockSpec((PEER_TB, D), lambda i: (i, 0))
    return pl.pallas_call(
        _peer_expert_kernel,
        grid=(T // PEER_TB,),
        in_specs=[
            pl.BlockSpec((PEER_TB, P_PAIRS), lambda i: (i, 0), memory_space=pltpu.SMEM),
            blk,
            pl.BlockSpec((PEER_TB, 2 * P_PAIRS), lambda i: (i, 0)),
            blk,
            pl.BlockSpec((None, 6, D), lambda i: ((i * PEER_TB) // SEG, 0, 0)),
            pl.BlockSpec(memory_space=pl.ANY),
        ],
        out_specs=blk,
        out_shape=jax.ShapeDtypeStruct((T, D), F32),
        scratch_shapes=[pltpu.VMEM((2, PEER_ROWS, D), jnp.uint32),
                        pltpu.SemaphoreType.DMA((2,))],
        compiler_params=_params("arbitrary"),
        name="peer_expert",
    )(idx, h, g, x, mod_l, table)


def _pack_expert_table(u, v):
    def halves(t):
        bits = lax.bitcast_convert_type(t.astype(BF16), jnp.uint16).astype(jnp.uint32)
        return bits[:, :D // 2] | (bits[:, D // 2:] << 16)
    return jnp.concatenate([halves(u), halves(v)], axis=1)


def kernel(x_prompt, x_sample, cache_k, cache_v, c, c_ctx, ada_w, ada_b, norm1_w, norm2_w,
           a_w_in, a_w_s, a_b_s, a_v_norm, a_w_out, b_w_qkv, b_q_norm, b_k_norm, b_rpb,
           b_w_out, peer_w_q, peer_sub_keys, peer_u, peer_v):
    n_prompt = x_prompt.shape[0]
    n_sample = x_sample.shape[0]
    x = jnp.concatenate([x_prompt.reshape(-1, D), x_sample.reshape(-1, D)], axis=0)

    cond8 = jnp.zeros((8, D), F32).at[0].set(c_ctx).at[1:1 + n_sample].set(c)
    mod = _ada_mod(cond8, ada_w, ada_b)

    head_of = jnp.arange(D) // HEAD_DIM
    ind = (head_of[:, None] == jnp.arange(128)[None, :]).astype(BF16)
    indt = ind.T
    seg_id = jnp.arange(2 * P_PAIRS) // (2 * P_TOPK)
    seg_ones = (seg_id[:, None] == seg_id[None, :]).astype(BF16)

    new_k, new_v = [], []
    for i in range(DEPTH):
        j = i // 2
        mod_l = mod[i]
        n1 = norm1_w[i].reshape(1, D)
        if i % 2 == 0:
            bexp = jnp.repeat(a_b_s[j].T, D // A_GROUPS, axis=1)
            x = _mix_a(x, mod_l, n1, a_w_in[j].astype(BF16), a_w_s[j].astype(BF16), bexp,
                       a_v_norm[j].reshape(1, D), a_w_out[j].astype(BF16))
        else:
            qn = jnp.tile(b_q_norm[j], HEADS).reshape(1, D)
            kn = jnp.tile(b_k_norm[j], HEADS).reshape(1, D)
            q, k, v, kf, vf = _qkv(x, mod_l, n1, b_w_qkv[j].astype(BF16), qn, kn, ind, indt)
            new_k.append(kf[:SEG].reshape(n_prompt, SEQ, HEADS, HEAD_DIM))
            new_v.append(vf[:SEG].reshape(n_prompt, SEQ, HEADS, HEAD_DIM))
            w_out = b_w_out[j].astype(BF16)
            x_ctx = _ctx_attn(q, k, v, x, mod_l, w_out)
            kc = cache_k[:, j].reshape(n_sample, SEQ, D).astype(BF16)
            vc = cache_v[:, j].reshape(n_sample, SEQ, D).astype(BF16)
            x_lat = _na_attn(q, k, v, kc, vc, _na_bias_table(b_rpb[j]), x, mod_l, w_out)
            x = jnp.concatenate([x_ctx, x_lat], axis=0)
        h, idx, g = _peer_select(x, mod_l, norm2_w[i].reshape(1, D), peer_w_q[i].astype(BF16),
                                 peer_sub_keys[i].astype(BF16), seg_ones)
        x = _peer_expert(idx, h, g, x, mod_l, _pack_expert_table(peer_u[i], peer_v[i]))

    y_prompt = x[:SEG].reshape(x_prompt.shape)
    y_sample = x[SEG:].reshape(x_sample.shape)
    return (y_prompt, y_sample, jnp.stack(new_k, axis=1), jnp.stack(new_v, axis=1))
```

```python
import functools

import jax
import jax.numpy as jnp
import numpy as np
from jax import lax
from jax.experimental import pallas as pl
from jax.experimental.pallas import tpu as pltpu

F32 = jnp.float32
BF16 = jnp.bfloat16

D = 1024
DEPTH = 4
SEG = 4096
N_SEG = 3
T = N_SEG * SEG
SEQ = 256
GRID_W = 64
CHUNK = 128
A_GROUPS = 8
HEADS = 16
HEAD_DIM = 64
NA_ROWS = 8
NA_COLS = 16
P_HEADS = 8
P_HALF = 128
P_KEYS = 128
P_TOPK = 16
P_PAIRS = P_HEADS * P_TOPK
N_EXPERTS = P_KEYS * P_KEYS
EPS = 1e-6
NEG = -1e30

VMEM_LIMIT = 56 * 1024 * 1024

TM = 256
PEER_TB = 64
PEER_G = 8
PEER_ROWS = PEER_G * P_PAIRS


def _params(*sem):
    return pltpu.CompilerParams(dimension_semantics=sem, vmem_limit_bytes=VMEM_LIMIT)


def _dot(a, b):
    return jnp.dot(a, b, preferred_element_type=F32)


def _dot_nt(a, b):
    return lax.dot_general(a, b, (((1,), (1,)), ((), ())), preferred_element_type=F32)


def _split(a):
    hi = a.astype(BF16)
    lo = (a - hi.astype(F32)).astype(BF16)
    return hi, lo


def _dot_f32(a, b):
    ah, al = _split(a)
    bh, bl = _split(b)
    return _dot(ah, bh) + _dot(ah, bl) + _dot(al, bh)


def _dot_f32_exact_rhs(a, b_bf16):
    ah, al = _split(a)
    return _dot(ah, b_bf16) + _dot(al, b_bf16)


def _rms_mod(x, norm_w, scale, shift):
    y = x * lax.rsqrt(jnp.mean(x * x, axis=-1, keepdims=True) + EPS)
    return (y * norm_w) * (1.0 + scale) + shift


def _seg_of_block(tm):
    return lambda i: (i * tm) // SEG


def _ada_kernel(cond_ref, w_ref, b_ref, o_ref):
    c = cond_ref[...]
    s = c * jax.nn.sigmoid(c)
    o_ref[...] = _dot_f32(s, w_ref[...]) + b_ref[...]


def _ada_mod(cond8, ada_w, ada_b):
    tn = 1024
    n = 6 * D
    out = pl.pallas_call(
        _ada_kernel,
        grid=(DEPTH, n // tn),
        in_specs=[
            pl.BlockSpec((8, D), lambda l, j: (0, 0)),
            pl.BlockSpec((None, D, tn), lambda l, j: (l, 0, j)),
            pl.BlockSpec((None, 1, tn), lambda l, j: (l, 0, j)),
        ],
        out_specs=pl.BlockSpec((None, 8, tn), lambda l, j: (l, 0, j)),
        out_shape=jax.ShapeDtypeStruct((DEPTH, 8, n), F32),
        compiler_params=_params("parallel", "parallel"),
        name="ada_mod",
    )(cond8, ada_w, ada_b.reshape(DEPTH, 1, n))
    return out.reshape(DEPTH, 8, 6, D)


def _mix_a_kernel(x_ref, mod_ref, n1_ref, win_ref, ws_ref, bexp_ref, vn_ref, wout_ref, o_ref):
    x = x_ref[...]
    h = _rms_mod(x, n1_ref[...], mod_ref[1:2, :], mod_ref[0:1, :])
    z = jax.nn.gelu(_dot(h.astype(BF16), win_ref[...]))
    u = z[:, :D]
    v = z[:, D:]
    v = v * lax.rsqrt(jnp.mean(v * v, axis=-1, keepdims=True) + EPS) * vn_ref[...]
    vb = v.astype(BF16)
    rows = []
    for c in range(TM // CHUNK):
        cols = []
        for g in range(A_GROUPS):
            vg = vb[c * CHUNK:(c + 1) * CHUNK, g * 128:(g + 1) * 128]
            cols.append(_dot(ws_ref[g], vg))
        rows.append(jnp.concatenate(cols, axis=1) + bexp_ref[...])
    sv = jnp.concatenate(rows, axis=0)
    mix = _dot((u * sv).astype(BF16), wout_ref[...])
    o_ref[...] = x + mod_ref[2:3, :] * mix


def _mix_a(x, mod_l, n1, w_in, w_s, bexp, vn, w_out):
    const2 = lambda i: (0, 0)
    return pl.pallas_call(
        _mix_a_kernel,
        grid=(T // TM,),
        in_specs=[
            pl.BlockSpec((TM, D), lambda i: (i, 0)),
            pl.BlockSpec((None, 6, D), lambda i: ((i * TM) // SEG, 0, 0)),
            pl.BlockSpec((1, D), const2),
            pl.BlockSpec((D, 2 * D), const2),
            pl.BlockSpec((A_GROUPS, CHUNK, CHUNK), lambda i: (0, 0, 0)),
            pl.BlockSpec((CHUNK, D), const2),
            pl.BlockSpec((1, D), const2),
            pl.BlockSpec((D, D), const2),
        ],
        out_specs=pl.BlockSpec((TM, D), lambda i: (i, 0)),
        out_shape=jax.ShapeDtypeStruct((T, D), F32),
        compiler_params=_params("parallel"),
        name="mix_a",
    )(x, mod_l, n1, w_in, w_s, bexp, vn, w_out)


def _qkv_kernel(x_ref, mod_ref, n1_ref, w_ref, qn_ref, kn_ref, ind_ref, indt_ref,
                q_o, k_o, v_o, kf_o, vf_o):
    x = x_ref[...]
    h = _rms_mod(x, n1_ref[...], mod_ref[1:2, :], mod_ref[0:1, :])
    qkv = _dot(h.astype(BF16), w_ref[...])

    def head_norm(t, w):
        ss = _dot_f32_exact_rhs(t * t, ind_ref[...])
        inv = lax.rsqrt(ss * (1.0 / HEAD_DIM) + EPS)
        return t * _dot_f32_exact_rhs(inv, indt_ref[...]) * w

    q = head_norm(qkv[:, :D], qn_ref[...])
    k = head_norm(qkv[:, D:2 * D], kn_ref[...])
    v = qkv[:, 2 * D:]
    q_o[...] = (q * (HEAD_DIM ** -0.5)).astype(BF16)
    k_o[...] = k.astype(BF16)
    v_o[...] = v.astype(BF16)
    kf_o[...] = k
    vf_o[...] = v


def _qkv(x, mod_l, n1, w_qkv, qn, kn, ind, indt):
    const2 = lambda i: (0, 0)
    blk = pl.BlockSpec((TM, D), lambda i: (i, 0))
    return pl.pallas_call(
        _qkv_kernel,
        grid=(T // TM,),
        in_specs=[
            blk,
            pl.BlockSpec((None, 6, D), lambda i: ((i * TM) // SEG, 0, 0)),
            pl.BlockSpec((1, D), const2),
            pl.BlockSpec((D, 3 * D), const2),
            pl.BlockSpec((1, D), const2),
            pl.BlockSpec((1, D), const2),
            pl.BlockSpec((D, 128), const2),
            pl.BlockSpec((128, D), const2),
        ],
        out_specs=[blk, blk, blk, blk, blk],
        out_shape=[jax.ShapeDtypeStruct((T, D), BF16)] * 3 + [jax.ShapeDtypeStruct((T, D), F32)] * 2,
        compiler_params=_params("parallel"),
        name="qkv",
    )(x, mod_l, n1, w_qkv, qn, kn, ind, indt)


def _head_pair_attention(q2, score_parts):
    lane = lax.broadcasted_iota(jnp.int32, (1, 128), 1)
    lo = lane < HEAD_DIM
    out = None
    for sub in range(2):
        keep = lo if sub == 0 else jnp.logical_not(lo)
        qm = jnp.where(keep, q2, jnp.zeros_like(q2))
        scores = []
        for k2, _, bias_fn in score_parts:
            s = _dot_nt(qm, k2)
            b = bias_fn(sub)
            scores.append(s if b is None else s + b)
        m = scores[0].max(axis=-1, keepdims=True)
        for s in scores[1:]:
            m = jnp.maximum(m, s.max(axis=-1, keepdims=True))
        den = None
        acc = None
        for s, (_, v2, _) in zip(scores, score_parts):
            p = jnp.exp(s - m)
            d = p.sum(axis=-1, keepdims=True)
            o = _dot(p.astype(BF16), v2)
            den = d if den is None else den + d
            acc = o if acc is None else acc + o
        o = acc / den
        out = o if sub == 0 else jnp.where(lo, out, o)
    return out


def _ctx_attn_kernel(q_ref, k_ref, v_ref, x_ref, mod_ref, wout_ref, o_ref):
    outs = []
    for hp in range(HEADS // 2):
        sl = slice(hp * 128, (hp + 1) * 128)
        outs.append(_head_pair_attention(
            q_ref[:, sl], [(k_ref[:, sl], v_ref[:, sl], lambda sub: None)]))
    o = jnp.concatenate(outs, axis=1).astype(BF16)
    o_ref[...] = x_ref[...] + mod_ref[2:3, :] * _dot(o, wout_ref[...])


def _ctx_attn(q, k, v, x, mod_l, w_out):
    n = SEG // SEQ
    blk = pl.BlockSpec((SEQ, D), lambda b: (b, 0))
    return pl.pallas_call(
        _ctx_attn_kernel,
        grid=(n,),
        in_specs=[blk, blk, blk, blk,
                  pl.BlockSpec((None, 6, D), lambda b: (0, 0, 0)),
                  pl.BlockSpec((D, D), lambda b: (0, 0))],
        out_specs=blk,
        out_shape=jax.ShapeDtypeStruct((SEG, D), F32),
        compiler_params=_params("parallel"),
        name="ctx_attn",
    )(q, k, v, x, mod_l, w_out)


def _na_window_start(r):
    rows = SEG // GRID_W
    return jnp.clip(r - NA_ROWS // 2, 0, rows - NA_ROWS)


def _na_attn_kernel(q_ref, k_ref, v_ref, kc_ref, vc_ref, bias_ref, x_ref, mod_ref, wout_ref, o_ref):
    r = pl.program_id(1)
    start = pl.multiple_of(_na_window_start(r) * GRID_W, GRID_W)
    n_loc = NA_ROWS * GRID_W
    outs = []
    for hp in range(HEADS // 2):
        sl = slice(hp * 128, (hp + 1) * 128)
        kw = k_ref[pl.ds(start, n_loc), sl]
        vw = v_ref[pl.ds(start, n_loc), sl]
        outs.append(_head_pair_attention(
            q_ref[:, sl],
            [(kw, vw, lambda sub, hp=hp: bias_ref[2 * hp + sub]),
             (kc_ref[:, sl], vc_ref[:, sl], lambda sub: None)]))
    o = jnp.concatenate(outs, axis=1).astype(BF16)
    o_ref[...] = x_ref[...] + mod_ref[2:3, :] * _dot(o, wout_ref[...])


def _na_attn(q, k, v, kc, vc, bias, x, mod_l, w_out):
    rows = SEG // GRID_W
    nb = (T - SEG) // SEG
    qblk = pl.BlockSpec((GRID_W, D), lambda b, r: (rows + b * rows + r, 0))
    img = pl.BlockSpec((SEG, D), lambda b, r: (1 + b, 0))
    ctx = pl.BlockSpec((None, SEQ, D), lambda b, r: (b, 0, 0))

    def bias_map(b, r):
        return (_na_window_start(r) - r + NA_ROWS - 1, 0, 0, 0)

    return pl.pallas_call(
        _na_attn_kernel,
        grid=(nb, rows),
        in_specs=[qblk, img, img, ctx, ctx,
                  pl.BlockSpec((None, HEADS, GRID_W, NA_ROWS * GRID_W), bias_map),
                  qblk,
                  pl.BlockSpec((None, 6, D), lambda b, r: (1 + b, 0, 0)),
                  pl.BlockSpec((D, D), lambda b, r: (0, 0))],
        out_specs=pl.BlockSpec((GRID_W, D), lambda b, r: (b * rows + r, 0)),
        out_shape=jax.ShapeDtypeStruct((T - SEG, D), F32),
        compiler_params=_params("parallel", "arbitrary"),
        name="na_attn",
    )(q, k, v, kc, vc, bias, x, mod_l, w_out)


def _na_bias_table(rpb):
    qc = np.arange(GRID_W)
    c0 = np.clip(qc - NA_COLS // 2, 0, GRID_W - NA_COLS)
    kc = np.arange(GRID_W)
    inside = (kc[None, :] >= c0[:, None]) & (kc[None, :] < c0[:, None] + NA_COLS)
    dc = kc[None, :] - qc[:, None] + NA_COLS - 1
    cols = jnp.zeros(rpb.shape[:2] + (GRID_W, GRID_W), F32)
    for d in range(2 * NA_COLS - 1):
        cols = jnp.where(jnp.asarray(inside & (dc == d))[None, None], rpb[:, :, d, None, None], cols)
    cols = jnp.where(jnp.asarray(inside)[None, None], cols, NEG)
    tab = jnp.stack([cols[:, var:var + NA_ROWS] for var in range(NA_ROWS)], axis=0)
    tab = jnp.transpose(tab, (0, 1, 3, 2, 4))
    return tab.reshape(NA_ROWS, HEADS, GRID_W, NA_ROWS * GRID_W)


def _peer_select_kernel(x_ref, mod_ref, n2_ref, wq_ref, sk_ref, seg_ref,
                        h_o, idx_o, g_o, q_scr):
    x = x_ref[...]
    h = _rms_mod(x, n2_ref[...], mod_ref[4:5, :], mod_ref[3:4, :])
    h_o[...] = h
    q_scr[...] = _dot(h.astype(BF16), wq_ref[...])

    n_c = P_TOPK * P_TOPK
    lane_k = lax.broadcasted_iota(jnp.int32, (TM, P_KEYS), 1)
    lane_c = lax.broadcasted_iota(jnp.int32, (TM, n_c), 1)
    lane_kf = lane_k.astype(F32)
    lane_cf = lane_c.astype(F32)
    lane_hi = lane_c // P_TOPK
    lane_lo = lane_c % P_TOPK
    neg_inf = jnp.float32(-jnp.inf)

    def head(hd, carry):
        idx_acc, sc_acc, m0_acc = carry
        base = hd * 2 * P_HALF
        cand = []
        cidx = []
        for p in range(2):
            off = pl.multiple_of(base + p * P_HALF, P_HALF)
            qhp = q_scr[:, pl.ds(off, P_HALF)].astype(BF16)
            s = _dot_nt(qhp, sk_ref[p])
            pos = lane_hi if p == 0 else lane_lo
            cv = jnp.zeros((TM, n_c), F32)
            ci = jnp.zeros((TM, n_c), F32)
            for it in range(P_TOPK):
                m = s.max(axis=-1, keepdims=True)
                am = jnp.where(s == m, lane_kf, float(P_KEYS)).min(axis=-1, keepdims=True)
                s = jnp.where(lane_kf == am, neg_inf, s)
                here = pos == it
                cv = jnp.where(here, m, cv)
                ci = jnp.where(here, am, ci)
            cand.append(cv)
            cidx.append(ci)
        cs = cand[0] + cand[1]
        ce = cidx[0] * float(P_KEYS) + cidx[1]
        for it in range(P_TOPK):
            m = cs.max(axis=-1, keepdims=True)
            am = jnp.where(cs == m, lane_cf, float(n_c)).min(axis=-1, keepdims=True)
            hit = lane_cf == am
            e = jnp.where(hit, ce, -1.0).max(axis=-1, keepdims=True)
            cs = jnp.where(hit, neg_inf, cs)
            slot = hd * P_TOPK + it
            idx_acc = jnp.where(lane_k == slot, e, idx_acc)
            sc_acc = jnp.where(lane_c == 2 * slot, m, sc_acc)
            if it == 0:
                m0_acc = jnp.where(lane_c // (2 * P_TOPK) == hd, m, m0_acc)
        return idx_acc, sc_acc, m0_acc

    idx_acc, sc_acc, m0_acc = lax.fori_loop(
        0, P_HEADS, head,
        (jnp.zeros((TM, P_PAIRS), F32), jnp.zeros((TM, 2 * P_PAIRS), F32),
         jnp.zeros((TM, 2 * P_PAIRS), F32)))
    even = (lane_c % 2) == 0
    ex = jnp.where(even, jnp.exp(sc_acc - m0_acc), 0.0)
    den = _dot_f32_exact_rhs(ex, seg_ref[...])
    idx_o[...] = idx_acc.astype(jnp.int32)
    g_o[...] = ex / den


def _peer_select(x, mod_l, n2, wq, sk, seg_ones):
    const2 = lambda i: (0, 0)
    blk = pl.BlockSpec((TM, D), lambda i: (i, 0))
    return pl.pallas_call(
        _peer_select_kernel,
        grid=(T // TM,),
        in_specs=[
            blk,
            pl.BlockSpec((None, 6, D), lambda i: ((i * TM) // SEG, 0, 0)),
            pl.BlockSpec((1, D), const2),
            pl.BlockSpec((D, P_HEADS * 2 * P_HALF), const2),
            pl.BlockSpec((2, P_KEYS, P_HALF), lambda i: (0, 0, 0)),
            pl.BlockSpec((2 * P_PAIRS, 2 * P_PAIRS), const2),
        ],
        out_specs=[blk,
                   pl.BlockSpec((TM, P_PAIRS), lambda i: (i, 0)),
                   pl.BlockSpec((TM, 2 * P_PAIRS), lambda i: (i, 0))],
        out_shape=[jax.ShapeDtypeStruct((T, D), F32),
                   jax.ShapeDtypeStruct((T, P_PAIRS), jnp.int32),
                   jax.ShapeDtypeStruct((T, 2 * P_PAIRS), F32)],
        scratch_shapes=[pltpu.VMEM((TM, P_HEADS * 2 * P_HALF), F32)],
        compiler_params=_params("parallel"),
        name="peer_select",
    )(x, mod_l, n2, wq, sk, seg_ones)


PEER_LANES = 2 * P_HEADS * P_PAIRS


def _peer_expert_kernel(idx_ref, h_ref, g_ref, x_ref, gate_ref, e2_ref, tab_ref, o_ref,
                        buf, grep, sem):
    n_batches = PEER_TB // PEER_G

    def gather_wait(slot):
        pltpu.make_async_copy(tab_ref.at[pl.ds(0, PEER_ROWS)], buf.at[slot], sem.at[slot]).wait()

    def gather_start(b, slot):
        def token(j, carry):
            t = b * PEER_G + j
            for k in range(P_PAIRS):
                e = idx_ref[t, k]
                pltpu.make_async_copy(
                    tab_ref.at[e],
                    buf.at[slot, j * P_PAIRS + k],
                    sem.at[slot]).start()
            return carry
        lax.fori_loop(0, PEER_G, token, 0)

    gather_start(0, 0)

    grep[...] = _dot_f32_exact_rhs(g_ref[...], e2_ref[...])

    sub = lax.broadcasted_iota(jnp.int32, (8, PEER_LANES), 0)
    lane = lax.broadcasted_iota(jnp.int32, (8, PEER_LANES), 1)
    is_u = (lane % 16) == sub
    is_v = (lane % 16) == sub + 8

    def token_rows(slot, j):
        tiles = buf[slot, j * P_PAIRS:(j + 1) * P_PAIRS]
        return pltpu.bitcast(tiles.reshape(P_PAIRS * 8, 128), BF16)

    def batch(b, carry):
        slot = b % 2

        @pl.when(b + 1 < n_batches)
        def _():
            gather_start(b + 1, 1 - slot)

        gather_wait(slot)
        t0 = pl.multiple_of(b * PEER_G, PEER_G)
        part = jnp.zeros((PEER_G, PEER_LANES), F32)
        for j in range(PEER_G):
            hm = h_ref[pl.ds(pl.multiple_of((t0 + j) * 8, 8), 8), :].astype(BF16)
            r = _dot_nt(hm, token_rows(slot, j))
            c = jnp.sum(jnp.where(is_u, r, 0.0), axis=0, keepdims=True)
            part = jnp.where(sub == j, c, part)
        a = part
        for k in (1, 2, 4, 8):
            up = pltpu.roll(a, k, 1)
            down = pltpu.roll(a, PEER_LANES - k, 1)
            a = a + jnp.where((lane & k) != 0, up, down)
        w = jax.nn.gelu(a) * grep[pl.ds(t0, PEER_G), :]
        for j in range(PEER_G):
            lj = jnp.where(is_v, jnp.broadcast_to(w[j:j + 1, :], (8, PEER_LANES)), 0.0)
            y = _dot(lj.astype(BF16), token_rows(slot, j))
            rows = pl.ds(pl.multiple_of((t0 + j) * 8, 8), 8)
            o_ref[rows, :] = x_ref[rows, :] + gate_ref[...] * y
        return carry

    lax.fori_loop(0, n_batches, batch, 0)


def _peer_expert(idx, h_tiles, g, x_tiles, gate_tiles, e2, table):
    blk = pl.BlockSpec((PEER_TB * 8, 128), lambda i: (i, 0))
    return pl.pallas_call(
        _peer_expert_kernel,
        grid=(T // PEER_TB,),
        in_specs=[
            pl.BlockSpec((PEER_TB, P_PAIRS), lambda i: (i, 0), memory_space=pltpu.SMEM),
            blk,
            pl.BlockSpec((PEER_TB, 2 * P_PAIRS), lambda i: (i, 0)),
            blk,
            pl.BlockSpec((None, 8, 128), lambda i: ((i * PEER_TB) // SEG, 0, 0)),
            pl.BlockSpec((2 * P_PAIRS, PEER_LANES), lambda i: (0, 0)),
            pl.BlockSpec(memory_space=pl.ANY),
        ],
        out_specs=blk,
        out_shape=jax.ShapeDtypeStruct((T * 8, 128), F32),
        scratch_shapes=[pltpu.VMEM((2, PEER_ROWS, 8, 128), jnp.uint32),
                        pltpu.VMEM((PEER_TB, PEER_LANES), F32),
                        pltpu.SemaphoreType.DMA((2,))],
        compiler_params=_params("arbitrary"),
        name="peer_expert",
    )(idx, h_tiles, g, x_tiles, gate_tiles, e2, table)


def _pack_expert_table(u, v):
    def tiles(t):
        bits = lax.bitcast_convert_type(t.astype(BF16), jnp.uint16).astype(jnp.uint32)
        bits = bits.reshape(N_EXPERTS, 4, 2, 128)
        return bits[:, :, 0, :] | (bits[:, :, 1, :] << 16)
    return jnp.concatenate([tiles(u), tiles(v)], axis=1)


def kernel(x_prompt, x_sample, cache_k, cache_v, c, c_ctx, ada_w, ada_b, norm1_w, norm2_w,
           a_w_in, a_w_s, a_b_s, a_v_norm, a_w_out, b_w_qkv, b_q_norm, b_k_norm, b_rpb,
           b_w_out, peer_w_q, peer_sub_keys, peer_u, peer_v):
    n_prompt = x_prompt.shape[0]
    n_sample = x_sample.shape[0]
    x = jnp.concatenate([x_prompt.reshape(-1, D), x_sample.reshape(-1, D)], axis=0)

    cond8 = jnp.zeros((8, D), F32).at[0].set(c_ctx).at[1:1 + n_sample].set(c)
    mod = _ada_mod(cond8, ada_w, ada_b)

    head_of = jnp.arange(D) // HEAD_DIM
    ind = (head_of[:, None] == jnp.arange(128)[None, :]).astype(BF16)
    indt = ind.T
    seg_id = jnp.arange(2 * P_PAIRS) // (2 * P_TOPK)
    seg_ones = (seg_id[:, None] == seg_id[None, :]).astype(BF16)
    pair_expand = (jnp.arange(2 * P_PAIRS)[:, None] * 8 == (jnp.arange(PEER_LANES)[None, :] // 16) * 16
                   ).astype(BF16)

    new_k, new_v = [], []
    for i in range(DEPTH):
        j = i // 2
        mod_l = mod[i]
        n1 = norm1_w[i].reshape(1, D)
        if i % 2 == 0:
            bexp = jnp.repeat(a_b_s[j].T, D // A_GROUPS, axis=1)
            x = _mix_a(x, mod_l, n1, a_w_in[j].astype(BF16), a_w_s[j].astype(BF16), bexp,
                       a_v_norm[j].reshape(1, D), a_w_out[j].astype(BF16))
        else:
            qn = jnp.tile(b_q_norm[j], HEADS).reshape(1, D)
            kn = jnp.tile(b_k_norm[j], HEADS).reshape(1, D)
            q, k, v, kf, vf = _qkv(x, mod_l, n1, b_w_qkv[j].astype(BF16), qn, kn, ind, indt)
            new_k.append(kf[:SEG].reshape(n_prompt, SEQ, HEADS, HEAD_DIM))
            new_v.append(vf[:SEG].reshape(n_prompt, SEQ, HEADS, HEAD_DIM))
            w_out = b_w_out[j].astype(BF16)
            x_ctx = _ctx_attn(q, k, v, x, mod_l, w_out)
            kc = cache_k[:, j].reshape(n_sample, SEQ, D).astype(BF16)
            vc = cache_v[:, j].reshape(n_sample, SEQ, D).astype(BF16)
            x_lat = _na_attn(q, k, v, kc, vc, _na_bias_table(b_rpb[j]), x, mod_l, w_out)
            x = jnp.concatenate([x_ctx, x_lat], axis=0)
        h, idx, g = _peer_select(x, mod_l, norm2_w[i].reshape(1, D), peer_w_q[i].astype(BF16),
                                 peer_sub_keys[i].astype(BF16), seg_ones)
        x = _peer_expert(idx, h.reshape(T * 8, 128), g, x.reshape(T * 8, 128),
                         mod_l[:, 5].reshape(8, 8, 128), pair_expand,
                         _pack_expert_table(peer_u[i], peer_v[i])).reshape(T, D)

    y_prompt = x[:SEG].reshape(x_prompt.shape)
    y_sample = x[SEG:].reshape(x_sample.shape)
    return (y_prompt, y_sample, jnp.stack(new_k, axis=1), jnp.stack(new_v, axis=1))
```

```python
import functools

import jax
import jax.numpy as jnp
import numpy as np
from jax import lax
from jax.experimental import pallas as pl
from jax.experimental.pallas import tpu as pltpu

F32 = jnp.float32
BF16 = jnp.bfloat16

D = 1024
DEPTH = 4
SEG = 4096
N_SEG = 3
T = N_SEG * SEG
SEQ = 256
GRID_W = 64
CHUNK = 128
A_GROUPS = 8
HEADS = 16
HEAD_DIM = 64
NA_ROWS = 8
NA_COLS = 16
P_HEADS = 8
P_HALF = 128
P_KEYS = 128
P_TOPK = 16
P_PAIRS = P_HEADS * P_TOPK
N_EXPERTS = P_KEYS * P_KEYS
EPS = 1e-6
NEG = -1e30

VMEM_LIMIT = 56 * 1024 * 1024

TM = 256
PEER_TB = 64
PEER_G = 8
PEER_ROWS = PEER_G * P_PAIRS


def _params(*sem):
    return pltpu.CompilerParams(dimension_semantics=sem, vmem_limit_bytes=VMEM_LIMIT)


def _dot(a, b):
    return jnp.dot(a, b, preferred_element_type=F32)


def _dot_nt(a, b):
    return lax.dot_general(a, b, (((1,), (1,)), ((), ())), preferred_element_type=F32)


def _split(a):
    hi = a.astype(BF16)
    lo = (a - hi.astype(F32)).astype(BF16)
    return hi, lo


def _dot_f32(a, b):
    ah, al = _split(a)
    bh, bl = _split(b)
    return _dot(ah, bh) + _dot(ah, bl) + _dot(al, bh)


def _dot_f32_exact_rhs(a, b_bf16):
    ah, al = _split(a)
    return _dot(ah, b_bf16) + _dot(al, b_bf16)


def _rms_mod(x, norm_w, scale, shift):
    y = x * lax.rsqrt(jnp.mean(x * x, axis=-1, keepdims=True) + EPS)
    return (y * norm_w) * (1.0 + scale) + shift


def _seg_of_block(tm):
    return lambda i: (i * tm) // SEG


def _ada_kernel(cond_ref, w_ref, b_ref, o_ref):
    c = cond_ref[...]
    s = c * jax.nn.sigmoid(c)
    o_ref[...] = _dot_f32(s, w_ref[...]) + b_ref[...]


def _ada_mod(cond8, ada_w, ada_b):
    tn = 1024
    n = 6 * D
    out = pl.pallas_call(
        _ada_kernel,
        grid=(DEPTH, n // tn),
        in_specs=[
            pl.BlockSpec((8, D), lambda l, j: (0, 0)),
            pl.BlockSpec((None, D, tn), lambda l, j: (l, 0, j)),
            pl.BlockSpec((None, 1, tn), lambda l, j: (l, 0, j)),
        ],
        out_specs=pl.BlockSpec((None, 8, tn), lambda l, j: (l, 0, j)),
        out_shape=jax.ShapeDtypeStruct((DEPTH, 8, n), F32),
        compiler_params=_params("parallel", "parallel"),
        name="ada_mod",
    )(cond8, ada_w, ada_b.reshape(DEPTH, 1, n))
    return out.reshape(DEPTH, 8, 6, D)


def _mix_a_kernel(x_ref, mod_ref, n1_ref, win_ref, ws_ref, bexp_ref, vn_ref, wout_ref, o_ref):
    x = x_ref[...]
    h = _rms_mod(x, n1_ref[...], mod_ref[1:2, :], mod_ref[0:1, :])
    z = jax.nn.gelu(_dot(h.astype(BF16), win_ref[...]))
    u = z[:, :D]
    v = z[:, D:]
    v = v * lax.rsqrt(jnp.mean(v * v, axis=-1, keepdims=True) + EPS) * vn_ref[...]
    vb = v.astype(BF16)
    rows = []
    for c in range(TM // CHUNK):
        cols = []
        for g in range(A_GROUPS):
            vg = vb[c * CHUNK:(c + 1) * CHUNK, g * 128:(g + 1) * 128]
            cols.append(_dot(ws_ref[g], vg))
        rows.append(jnp.concatenate(cols, axis=1) + bexp_ref[...])
    sv = jnp.concatenate(rows, axis=0)
    mix = _dot((u * sv).astype(BF16), wout_ref[...])
    o_ref[...] = x + mod_ref[2:3, :] * mix


def _mix_a(x, mod_l, n1, w_in, w_s, bexp, vn, w_out):
    const2 = lambda i: (0, 0)
    return pl.pallas_call(
        _mix_a_kernel,
        grid=(T // TM,),
        in_specs=[
            pl.BlockSpec((TM, D), lambda i: (i, 0)),
            pl.BlockSpec((None, 6, D), lambda i: ((i * TM) // SEG, 0, 0)),
            pl.BlockSpec((1, D), const2),
            pl.BlockSpec((D, 2 * D), const2),
            pl.BlockSpec((A_GROUPS, CHUNK, CHUNK), lambda i: (0, 0, 0)),
            pl.BlockSpec((CHUNK, D), const2),
            pl.BlockSpec((1, D), const2),
            pl.BlockSpec((D, D), const2),
        ],
        out_specs=pl.BlockSpec((TM, D), lambda i: (i, 0)),
        out_shape=jax.ShapeDtypeStruct((T, D), F32),
        compiler_params=_params("parallel"),
        name="mix_a",
    )(x, mod_l, n1, w_in, w_s, bexp, vn, w_out)


def _qkv_kernel(x_ref, mod_ref, n1_ref, w_ref, qn_ref, kn_ref, ind_ref, indt_ref,
                q_o, k_o, v_o, kf_o, vf_o):
    x = x_ref[...]
    h = _rms_mod(x, n1_ref[...], mod_ref[1:2, :], mod_ref[0:1, :])
    qkv = _dot(h.astype(BF16), w_ref[...])

    def head_norm(t, w):
        ss = _dot_f32_exact_rhs(t * t, ind_ref[...])
        inv = lax.rsqrt(ss * (1.0 / HEAD_DIM) + EPS)
        return t * _dot_f32_exact_rhs(inv, indt_ref[...]) * w

    q = head_norm(qkv[:, :D], qn_ref[...])
    k = head_norm(qkv[:, D:2 * D], kn_ref[...])
    v = qkv[:, 2 * D:]
    q_o[...] = (q * (HEAD_DIM ** -0.5)).astype(BF16)
    k_o[...] = k.astype(BF16)
    v_o[...] = v.astype(BF16)
    kf_o[...] = k
    vf_o[...] = v


def _qkv(x, mod_l, n1, w_qkv, qn, kn, ind, indt):
    const2 = lambda i: (0, 0)
    blk = pl.BlockSpec((TM, D), lambda i: (i, 0))
    return pl.pallas_call(
        _qkv_kernel,
        grid=(T // TM,),
        in_specs=[
            blk,
            pl.BlockSpec((None, 6, D), lambda i: ((i * TM) // SEG, 0, 0)),
            pl.BlockSpec((1, D), const2),
            pl.BlockSpec((D, 3 * D), const2),
            pl.BlockSpec((1, D), const2),
            pl.BlockSpec((1, D), const2),
            pl.BlockSpec((D, 128), const2),
            pl.BlockSpec((128, D), const2),
        ],
        out_specs=[blk, blk, blk, blk, blk],
        out_shape=[jax.ShapeDtypeStruct((T, D), BF16)] * 3 + [jax.ShapeDtypeStruct((T, D), F32)] * 2,
        compiler_params=_params("parallel"),
        name="qkv",
    )(x, mod_l, n1, w_qkv, qn, kn, ind, indt)


def _head_pair_attention(q2, score_parts):
    lane = lax.broadcasted_iota(jnp.int32, (1, 128), 1)
    lo = lane < HEAD_DIM
    out = None
    for sub in range(2):
        keep = lo if sub == 0 else jnp.logical_not(lo)
        qm = jnp.where(keep, q2, jnp.zeros_like(q2))
        scores = []
        for k2, _, bias_fn in score_parts:
            s = _dot_nt(qm, k2)
            b = bias_fn(sub)
            scores.append(s if b is None else s + b)
        m = scores[0].max(axis=-1, keepdims=True)
        for s in scores[1:]:
            m = jnp.maximum(m, s.max(axis=-1, keepdims=True))
        den = None
        acc = None
        for s, (_, v2, _) in zip(scores, score_parts):
            p = jnp.exp(s - m)
            d = p.sum(axis=-1, keepdims=True)
            o = _dot(p.astype(BF16), v2)
            den = d if den is None else den + d
            acc = o if acc is None else acc + o
        o = acc / den
        out = o if sub == 0 else jnp.where(lo, out, o)
    return out


def _ctx_attn_kernel(q_ref, k_ref, v_ref, x_ref, mod_ref, wout_ref, o_ref):
    outs = []
    for hp in range(HEADS // 2):
        sl = slice(hp * 128, (hp + 1) * 128)
        outs.append(_head_pair_attention(
            q_ref[:, sl], [(k_ref[:, sl], v_ref[:, sl], lambda sub: None)]))
    o = jnp.concatenate(outs, axis=1).astype(BF16)
    o_ref[...] = x_ref[...] + mod_ref[2:3, :] * _dot(o, wout_ref[...])


def _ctx_attn(q, k, v, x, mod_l, w_out):
    n = SEG // SEQ
    blk = pl.BlockSpec((SEQ, D), lambda b: (b, 0))
    return pl.pallas_call(
        _ctx_attn_kernel,
        grid=(n,),
        in_specs=[blk, blk, blk, blk,
                  pl.BlockSpec((None, 6, D), lambda b: (0, 0, 0)),
                  pl.BlockSpec((D, D), lambda b: (0, 0))],
        out_specs=blk,
        out_shape=jax.ShapeDtypeStruct((SEG, D), F32),
        compiler_params=_params("parallel"),
        name="ctx_attn",
    )(q, k, v, x, mod_l, w_out)


def _na_window_start(r):
    rows = SEG // GRID_W
    return jnp.clip(r - NA_ROWS // 2, 0, rows - NA_ROWS)


def _na_attn_kernel(q_ref, k_ref, v_ref, kc_ref, vc_ref, bias_ref, x_ref, mod_ref, wout_ref, o_ref):
    r = pl.program_id(1)
    start = pl.multiple_of(_na_window_start(r) * GRID_W, GRID_W)
    n_loc = NA_ROWS * GRID_W
    outs = []
    for hp in range(HEADS // 2):
        sl = slice(hp * 128, (hp + 1) * 128)
        kw = k_ref[pl.ds(start, n_loc), sl]
        vw = v_ref[pl.ds(start, n_loc), sl]
        outs.append(_head_pair_attention(
            q_ref[:, sl],
            [(kw, vw, lambda sub, hp=hp: bias_ref[2 * hp + sub]),
             (kc_ref[:, sl], vc_ref[:, sl], lambda sub: None)]))
    o = jnp.concatenate(outs, axis=1).astype(BF16)
    o_ref[...] = x_ref[...] + mod_ref[2:3, :] * _dot(o, wout_ref[...])


def _na_attn(q, k, v, kc, vc, bias, x, mod_l, w_out):
    rows = SEG // GRID_W
    nb = (T - SEG) // SEG
    qblk = pl.BlockSpec((GRID_W, D), lambda b, r: (rows + b * rows + r, 0))
    img = pl.BlockSpec((SEG, D), lambda b, r: (1 + b, 0))
    ctx = pl.BlockSpec((None, SEQ, D), lambda b, r: (b, 0, 0))

    def bias_map(b, r):
        return (_na_window_start(r) - r + NA_ROWS - 1, 0, 0, 0)

    return pl.pallas_call(
        _na_attn_kernel,
        grid=(nb, rows),
        in_specs=[qblk, img, img, ctx, ctx,
                  pl.BlockSpec((None, HEADS, GRID_W, NA_ROWS * GRID_W), bias_map),
                  qblk,
                  pl.BlockSpec((None, 6, D), lambda b, r: (1 + b, 0, 0)),
                  pl.BlockSpec((D, D), lambda b, r: (0, 0))],
        out_specs=pl.BlockSpec((GRID_W, D), lambda b, r: (b * rows + r, 0)),
        out_shape=jax.ShapeDtypeStruct((T - SEG, D), F32),
        compiler_params=_params("parallel", "arbitrary"),
        name="na_attn",
    )(q, k, v, kc, vc, bias, x, mod_l, w_out)


def _na_bias_table(rpb):
    qc = np.arange(GRID_W)
    c0 = np.clip(qc - NA_COLS // 2, 0, GRID_W - NA_COLS)
    kc = np.arange(GRID_W)
    inside = (kc[None, :] >= c0[:, None]) & (kc[None, :] < c0[:, None] + NA_COLS)
    dc = kc[None, :] - qc[:, None] + NA_COLS - 1
    cols = jnp.zeros(rpb.shape[:2] + (GRID_W, GRID_W), F32)
    for d in range(2 * NA_COLS - 1):
        cols = jnp.where(jnp.asarray(inside & (dc == d))[None, None], rpb[:, :, d, None, None], cols)
    cols = jnp.where(jnp.asarray(inside)[None, None], cols, NEG)
    tab = jnp.stack([cols[:, var:var + NA_ROWS] for var in range(NA_ROWS)], axis=0)
    tab = jnp.transpose(tab, (0, 1, 3, 2, 4))
    return tab.reshape(NA_ROWS, HEADS, GRID_W, NA_ROWS * GRID_W)


def _peer_select_kernel(x_ref, mod_ref, n2_ref, wqt_ref, sk_ref, h_o, idx_o, g_o, q_scr):
    x = x_ref[...]
    h = _rms_mod(x, n2_ref[...], mod_ref[4:5, :], mod_ref[3:4, :])
    h_o[...] = h
    q_scr[...] = _dot_nt(wqt_ref[...], h.astype(BF16))

    n_c = P_TOPK * P_TOPK
    key_pos = lax.broadcasted_iota(jnp.int32, (P_KEYS, TM), 0).astype(F32)
    rank = lax.broadcasted_iota(jnp.int32, (P_TOPK, TM), 0)
    rank8 = lax.broadcasted_iota(jnp.int32, (8, TM), 0)
    cand_pos = jnp.concatenate(
        [rank] + [a * P_TOPK + rank8 for a in range(1, 8)] + [(8 + rank8) * P_TOPK], axis=0).astype(F32)
    neg_inf = jnp.float32(-jnp.inf)

    def head(hd, carry):
        top_val, top_idx = [], []
        for p in range(2):
            off = pl.multiple_of(hd * 2 * P_HALF + p * P_HALF, P_HALF)
            s = _dot(sk_ref[p], q_scr[pl.ds(off, P_HALF), :].astype(BF16))
            vals, idxs = [], []
            for it in range(P_TOPK):
                m = jnp.max(s, axis=0, keepdims=True)
                am = jnp.min(jnp.where(s == m, key_pos, float(P_KEYS)), axis=0, keepdims=True)
                s = jnp.where(key_pos == am, neg_inf, s)
                vals.append(m)
                idxs.append(am)
            top_val.append(vals)
            top_idx.append(idxs)
        s2 = jnp.zeros((P_TOPK, TM), F32)
        i2 = jnp.zeros((P_TOPK, TM), F32)
        for it in range(P_TOPK):
            s2 = jnp.where(rank == it, top_val[1][it], s2)
            i2 = jnp.where(rank == it, top_idx[1][it], i2)
        s1_hi = jnp.zeros((8, TM), F32)
        i1_hi = jnp.zeros((8, TM), F32)
        for it in range(8, P_TOPK):
            s1_hi = jnp.where(rank8 == it - 8, top_val[0][it], s1_hi)
            i1_hi = jnp.where(rank8 == it - 8, top_idx[0][it], i1_hi)
        cs_parts = [top_val[0][0] + s2]
        ce_parts = [top_idx[0][0] * float(P_KEYS) + i2]
        for a in range(1, 8):
            live = rank8 < P_TOPK // (a + 1)
            cs_parts.append(jnp.where(live, top_val[0][a] + s2[:8], neg_inf))
            ce_parts.append(top_idx[0][a] * float(P_KEYS) + i2[:8])
        cs_parts.append(s1_hi + top_val[1][0])
        ce_parts.append(i1_hi * float(P_KEYS) + top_idx[1][0])
        cs = jnp.concatenate(cs_parts, axis=0)
        ce = jnp.concatenate(ce_parts, axis=0)
        sc = jnp.zeros((P_TOPK, TM), F32)
        ex_id = jnp.zeros((P_TOPK, TM), F32)
        best = None
        for it in range(P_TOPK):
            m = jnp.max(cs, axis=0, keepdims=True)
            am = jnp.min(jnp.where(cs == m, cand_pos, float(n_c)), axis=0, keepdims=True)
            hit = cand_pos == am
            e = jnp.max(jnp.where(hit, ce, -1.0), axis=0, keepdims=True)
            cs = jnp.where(hit, neg_inf, cs)
            sc = jnp.where(rank == it, m, sc)
            ex_id = jnp.where(rank == it, e, ex_id)
            if it == 0:
                best = m
        ex = jnp.exp(sc - best)
        rows = pl.ds(pl.multiple_of(hd * P_TOPK, P_TOPK), P_TOPK)
        idx_o[rows, :] = ex_id.astype(jnp.int32)
        g_o[rows, :] = ex / jnp.sum(ex, axis=0, keepdims=True)
        return carry

    lax.fori_loop(0, P_HEADS, head, 0)


def _peer_select(x, mod_l, n2, wqt, sk):
    const2 = lambda i: (0, 0)
    blk = pl.BlockSpec((TM, D), lambda i: (i, 0))
    tblk = pl.BlockSpec((P_PAIRS, TM), lambda i: (0, i))
    return pl.pallas_call(
        _peer_select_kernel,
        grid=(T // TM,),
        in_specs=[
            blk,
            pl.BlockSpec((None, 6, D), lambda i: ((i * TM) // SEG, 0, 0)),
            pl.BlockSpec((1, D), const2),
            pl.BlockSpec((P_HEADS * 2 * P_HALF, D), const2),
            pl.BlockSpec((2, P_KEYS, P_HALF), lambda i: (0, 0, 0)),
        ],
        out_specs=[blk, tblk, tblk],
        out_shape=[jax.ShapeDtypeStruct((T, D), F32),
                   jax.ShapeDtypeStruct((P_PAIRS, T), jnp.int32),
                   jax.ShapeDtypeStruct((P_PAIRS, T), F32)],
        scratch_shapes=[pltpu.VMEM((P_HEADS * 2 * P_HALF, TM), F32)],
        compiler_params=_params("parallel"),
        name="peer_select",
    )(x, mod_l, n2, wqt, sk)


PEER_LANES = 2 * P_HEADS * P_PAIRS


def _peer_expert_kernel(idx_ref, h_ref, g_ref, x_ref, gate_ref, e2_ref, tab_ref, o_ref,
                        buf, grep, sem):
    n_batches = PEER_TB // PEER_G

    def gather_wait(slot):
        pltpu.make_async_copy(tab_ref.at[pl.ds(0, PEER_ROWS)], buf.at[slot], sem.at[slot]).wait()

    def gather_start(b, slot):
        def token(j, carry):
            t = b * PEER_G + j
            for k in range(P_PAIRS):
                e = idx_ref[t, k]
                pltpu.make_async_copy(
                    tab_ref.at[e],
                    buf.at[slot, j * P_PAIRS + k],
                    sem.at[slot]).start()
            return carry
        lax.fori_loop(0, PEER_G, token, 0)

    gather_start(0, 0)

    grep[...] = _dot_f32_exact_rhs(g_ref[...], e2_ref[...])

    sub = lax.broadcasted_iota(jnp.int32, (8, PEER_LANES), 0)
    lane = lax.broadcasted_iota(jnp.int32, (8, PEER_LANES), 1)
    is_u = (lane % 16) == sub
    is_v = (lane % 16) == sub + 8

    def token_rows(slot, j):
        tiles = buf[slot, j * P_PAIRS:(j + 1) * P_PAIRS]
        return pltpu.bitcast(tiles.reshape(P_PAIRS * 8, 128), BF16)

    def batch(b, carry):
        slot = b % 2

        @pl.when(b + 1 < n_batches)
        def _():
            gather_start(b + 1, 1 - slot)

        gather_wait(slot)
        t0 = pl.multiple_of(b * PEER_G, PEER_G)
        part = jnp.zeros((PEER_G, PEER_LANES), F32)
        for j in range(PEER_G):
            hm = h_ref[pl.ds(pl.multiple_of((t0 + j) * 8, 8), 8), :].astype(BF16)
            r = _dot_nt(hm, token_rows(slot, j))
            c = jnp.sum(jnp.where(is_u, r, 0.0), axis=0, keepdims=True)
            part = jnp.where(sub == j, c, part)
        a = part
        for k in (1, 2, 4, 8):
            up = pltpu.roll(a, k, 1)
            down = pltpu.roll(a, PEER_LANES - k, 1)
            a = a + jnp.where((lane & k) != 0, up, down)
        w = jax.nn.gelu(a) * grep[pl.ds(t0, PEER_G), :]
        for j in range(PEER_G):
            lj = jnp.where(is_v, jnp.broadcast_to(w[j:j + 1, :], (8, PEER_LANES)), 0.0)
            y = _dot(lj.astype(BF16), token_rows(slot, j))
            rows = pl.ds(pl.multiple_of((t0 + j) * 8, 8), 8)
            o_ref[rows, :] = x_ref[rows, :] + gate_ref[...] * y
        return carry

    lax.fori_loop(0, n_batches, batch, 0)


def _peer_expert(idx, h_tiles, g, x_tiles, gate_tiles, e2, table):
    blk = pl.BlockSpec((PEER_TB * 8, 128), lambda i: (i, 0))
    return pl.pallas_call(
        _peer_expert_kernel,
        grid=(T // PEER_TB,),
        in_specs=[
            pl.BlockSpec((PEER_TB, P_PAIRS), lambda i: (i, 0), memory_space=pltpu.SMEM),
            blk,
            pl.BlockSpec((PEER_TB, P_PAIRS), lambda i: (i, 0)),
            blk,
            pl.BlockSpec((None, 8, 128), lambda i: ((i * PEER_TB) // SEG, 0, 0)),
            pl.BlockSpec((P_PAIRS, PEER_LANES), lambda i: (0, 0)),
            pl.BlockSpec(memory_space=pl.ANY),
        ],
        out_specs=blk,
        out_shape=jax.ShapeDtypeStruct((T * 8, 128), F32),
        scratch_shapes=[pltpu.VMEM((2, PEER_ROWS, 8, 128), jnp.uint32),
                        pltpu.VMEM((PEER_TB, PEER_LANES), F32),
                        pltpu.SemaphoreType.DMA((2,))],
        compiler_params=_params("arbitrary"),
        name="peer_expert",
    )(idx, h_tiles, g, x_tiles, gate_tiles, e2, table)


def _pack_expert_table(u, v):
    def tiles(t):
        bits = lax.bitcast_convert_type(t.astype(BF16), jnp.uint16).astype(jnp.uint32)
        bits = bits.reshape(N_EXPERTS, 4, 2, 128)
        return bits[:, :, 0, :] | (bits[:, :, 1, :] << 16)
    return jnp.concatenate([tiles(u), tiles(v)], axis=1)


def kernel(x_prompt, x_sample, cache_k, cache_v, c, c_ctx, ada_w, ada_b, norm1_w, norm2_w,
           a_w_in, a_w_s, a_b_s, a_v_norm, a_w_out, b_w_qkv, b_q_norm, b_k_norm, b_rpb,
           b_w_out, peer_w_q, peer_sub_keys, peer_u, peer_v):
    n_prompt = x_prompt.shape[0]
    n_sample = x_sample.shape[0]
    x = jnp.concatenate([x_prompt.reshape(-1, D), x_sample.reshape(-1, D)], axis=0)

    cond8 = jnp.zeros((8, D), F32).at[0].set(c_ctx).at[1:1 + n_sample].set(c)
    mod = _ada_mod(cond8, ada_w, ada_b)

    head_of = jnp.arange(D) // HEAD_DIM
    ind = (head_of[:, None] == jnp.arange(128)[None, :]).astype(BF16)
    indt = ind.T
    pair_expand = (jnp.arange(P_PAIRS)[:, None] == jnp.arange(PEER_LANES)[None, :] // 16).astype(BF16)

    new_k, new_v = [], []
    for i in range(DEPTH):
        j = i // 2
        mod_l = mod[i]
        n1 = norm1_w[i].reshape(1, D)
        if i % 2 == 0:
            bexp = jnp.repeat(a_b_s[j].T, D // A_GROUPS, axis=1)
            x = _mix_a(x, mod_l, n1, a_w_in[j].astype(BF16), a_w_s[j].astype(BF16), bexp,
                       a_v_norm[j].reshape(1, D), a_w_out[j].astype(BF16))
        else:
            qn = jnp.tile(b_q_norm[j], HEADS).reshape(1, D)
            kn = jnp.tile(b_k_norm[j], HEADS).reshape(1, D)
            q, k, v, kf, vf = _qkv(x, mod_l, n1, b_w_qkv[j].astype(BF16), qn, kn, ind, indt)
            new_k.append(kf[:SEG].reshape(n_prompt, SEQ, HEADS, HEAD_DIM))
            new_v.append(vf[:SEG].reshape(n_prompt, SEQ, HEADS, HEAD_DIM))
            w_out = b_w_out[j].astype(BF16)
            x_ctx = _ctx_attn(q, k, v, x, mod_l, w_out)
            kc = cache_k[:, j].reshape(n_sample, SEQ, D).astype(BF16)
            vc = cache_v[:, j].reshape(n_sample, SEQ, D).astype(BF16)
            x_lat = _na_attn(q, k, v, kc, vc, _na_bias_table(b_rpb[j]), x, mod_l, w_out)
            x = jnp.concatenate([x_ctx, x_lat], axis=0)
        h, idx_t, g_t = _peer_select(x, mod_l, norm2_w[i].reshape(1, D), peer_w_q[i].T.astype(BF16),
                                     peer_sub_keys[i].astype(BF16))
        x = _peer_expert(idx_t.T, h.reshape(T * 8, 128), g_t.T, x.reshape(T * 8, 128),
                         mod_l[:, 5].reshape(8, 8, 128), pair_expand,
                         _pack_expert_table(peer_u[i], peer_v[i])).reshape(T, D)

    y_prompt = x[:SEG].reshape(x_prompt.shape)
    y_sample = x[SEG:].reshape(x_sample.shape)
    return (y_prompt, y_sample, jnp.stack(new_k, axis=1), jnp.stack(new_v, axis=1))
```

```python
import functools

import jax
import jax.numpy as jnp
import numpy as np
from jax import lax
from jax.experimental import pallas as pl
from jax.experimental.pallas import tpu as pltpu

F32 = jnp.float32
BF16 = jnp.bfloat16

D = 1024
DEPTH = 4
SEG = 4096
N_SEG = 3
T = N_SEG * SEG
SEQ = 256
GRID_W = 64
CHUNK = 128
A_GROUPS = 8
HEADS = 16
HEAD_DIM = 64
NA_ROWS = 8
NA_COLS = 16
P_HEADS = 8
P_HALF = 128
P_KEYS = 128
P_TOPK = 16
P_PAIRS = P_HEADS * P_TOPK
N_EXPERTS = P_KEYS * P_KEYS
EPS = 1e-6
NEG = -1e30

VMEM_LIMIT = 56 * 1024 * 1024

TM = 256
PEER_TB = 64
PEER_G = 8
PEER_ROWS = PEER_G * P_PAIRS


def _params(*sem):
    return pltpu.CompilerParams(dimension_semantics=sem, vmem_limit_bytes=VMEM_LIMIT)


def _dot(a, b):
    return jnp.dot(a, b, preferred_element_type=F32)


def _dot_nt(a, b):
    return lax.dot_general(a, b, (((1,), (1,)), ((), ())), preferred_element_type=F32)


def _split(a):
    hi = a.astype(BF16)
    lo = (a - hi.astype(F32)).astype(BF16)
    return hi, lo


def _dot_f32(a, b):
    ah, al = _split(a)
    bh, bl = _split(b)
    return _dot(ah, bh) + _dot(ah, bl) + _dot(al, bh)


def _dot_f32_exact_rhs(a, b_bf16):
    ah, al = _split(a)
    return _dot(ah, b_bf16) + _dot(al, b_bf16)


def _rms_mod(x, norm_w, scale, shift):
    y = x * lax.rsqrt(jnp.mean(x * x, axis=-1, keepdims=True) + EPS)
    return (y * norm_w) * (1.0 + scale) + shift


def _seg_of_block(tm):
    return lambda i: (i * tm) // SEG


def _ada_kernel(cond_ref, w_ref, b_ref, o_ref):
    c = cond_ref[...]
    s = c * jax.nn.sigmoid(c)
    o_ref[...] = _dot_f32(s, w_ref[...]) + b_ref[...]


def _ada_mod(cond8, ada_w, ada_b):
    tn = 1024
    n = 6 * D
    out = pl.pallas_call(
        _ada_kernel,
        grid=(DEPTH, n // tn),
        in_specs=[
            pl.BlockSpec((8, D), lambda l, j: (0, 0)),
            pl.BlockSpec((None, D, tn), lambda l, j: (l, 0, j)),
            pl.BlockSpec((None, 1, tn), lambda l, j: (l, 0, j)),
        ],
        out_specs=pl.BlockSpec((None, 8, tn), lambda l, j: (l, 0, j)),
        out_shape=jax.ShapeDtypeStruct((DEPTH, 8, n), F32),
        compiler_params=_params("parallel", "parallel"),
        name="ada_mod",
    )(cond8, ada_w, ada_b.reshape(DEPTH, 1, n))
    return out.reshape(DEPTH, 8, 6, D)


def _mix_a_kernel(x_ref, mod_ref, n1_ref, win_ref, ws_ref, bexp_ref, vn_ref, wout_ref, o_ref):
    x = x_ref[...]
    h = _rms_mod(x, n1_ref[...], mod_ref[1:2, :], mod_ref[0:1, :])
    z = jax.nn.gelu(_dot(h.astype(BF16), win_ref[...]))
    u = z[:, :D]
    v = z[:, D:]
    v = v * lax.rsqrt(jnp.mean(v * v, axis=-1, keepdims=True) + EPS) * vn_ref[...]
    vb = v.astype(BF16)
    rows = []
    for c in range(TM // CHUNK):
        cols = []
        for g in range(A_GROUPS):
            vg = vb[c * CHUNK:(c + 1) * CHUNK, g * 128:(g + 1) * 128]
            cols.append(_dot(ws_ref[g], vg))
        rows.append(jnp.concatenate(cols, axis=1) + bexp_ref[...])
    sv = jnp.concatenate(rows, axis=0)
    mix = _dot((u * sv).astype(BF16), wout_ref[...])
    o_ref[...] = x + mod_ref[2:3, :] * mix


def _mix_a(x, mod_l, n1, w_in, w_s, bexp, vn, w_out):
    const2 = lambda i: (0, 0)
    return pl.pallas_call(
        _mix_a_kernel,
        grid=(T // TM,),
        in_specs=[
            pl.BlockSpec((TM, D), lambda i: (i, 0)),
            pl.BlockSpec((None, 6, D), lambda i: ((i * TM) // SEG, 0, 0)),
            pl.BlockSpec((1, D), const2),
            pl.BlockSpec((D, 2 * D), const2),
            pl.BlockSpec((A_GROUPS, CHUNK, CHUNK), lambda i: (0, 0, 0)),
            pl.BlockSpec((CHUNK, D), const2),
            pl.BlockSpec((1, D), const2),
            pl.BlockSpec((D, D), const2),
        ],
        out_specs=pl.BlockSpec((TM, D), lambda i: (i, 0)),
        out_shape=jax.ShapeDtypeStruct((T, D), F32),
        compiler_params=_params("parallel"),
        name="mix_a",
    )(x, mod_l, n1, w_in, w_s, bexp, vn, w_out)


def _qkv_kernel(x_ref, mod_ref, n1_ref, w_ref, qn_ref, kn_ref, ind_ref, indt_ref,
                q_o, k_o, v_o, kf_o, vf_o):
    x = x_ref[...]
    h = _rms_mod(x, n1_ref[...], mod_ref[1:2, :], mod_ref[0:1, :])
    qkv = _dot(h.astype(BF16), w_ref[...])

    def head_norm(t, w):
        ss = _dot_f32_exact_rhs(t * t, ind_ref[...])
        inv = lax.rsqrt(ss * (1.0 / HEAD_DIM) + EPS)
        return t * _dot_f32_exact_rhs(inv, indt_ref[...]) * w

    q = head_norm(qkv[:, :D], qn_ref[...])
    k = head_norm(qkv[:, D:2 * D], kn_ref[...])
    v = qkv[:, 2 * D:]
    q_o[...] = (q * (HEAD_DIM ** -0.5)).astype(BF16)
    k_o[...] = k.astype(BF16)
    v_o[...] = v.astype(BF16)
    kf_o[...] = k
    vf_o[...] = v


def _qkv(x, mod_l, n1, w_qkv, qn, kn, ind, indt):
    const2 = lambda i: (0, 0)
    blk = pl.BlockSpec((TM, D), lambda i: (i, 0))
    return pl.pallas_call(
        _qkv_kernel,
        grid=(T // TM,),
        in_specs=[
            blk,
            pl.BlockSpec((None, 6, D), lambda i: ((i * TM) // SEG, 0, 0)),
            pl.BlockSpec((1, D), const2),
            pl.BlockSpec((D, 3 * D), const2),
            pl.BlockSpec((1, D), const2),
            pl.BlockSpec((1, D), const2),
            pl.BlockSpec((D, 128), const2),
            pl.BlockSpec((128, D), const2),
        ],
        out_specs=[blk, blk, blk, blk, blk],
        out_shape=[jax.ShapeDtypeStruct((T, D), BF16)] * 3 + [jax.ShapeDtypeStruct((T, D), F32)] * 2,
        compiler_params=_params("parallel"),
        name="qkv",
    )(x, mod_l, n1, w_qkv, qn, kn, ind, indt)


def _head_pair_attention(q2, score_parts):
    lane = lax.broadcasted_iota(jnp.int32, (1, 128), 1)
    lo = lane < HEAD_DIM
    out = None
    for sub in range(2):
        keep = lo if sub == 0 else jnp.logical_not(lo)
        qm = jnp.where(keep, q2, jnp.zeros_like(q2))
        scores = []
        for k2, _, bias_fn in score_parts:
            s = _dot_nt(qm, k2)
            b = bias_fn(sub)
            scores.append(s if b is None else s + b)
        m = scores[0].max(axis=-1, keepdims=True)
        for s in scores[1:]:
            m = jnp.maximum(m, s.max(axis=-1, keepdims=True))
        den = None
        acc = None
        for s, (_, v2, _) in zip(scores, score_parts):
            p = jnp.exp(s - m)
            d = p.sum(axis=-1, keepdims=True)
            o = _dot(p.astype(BF16), v2)
            den = d if den is None else den + d
            acc = o if acc is None else acc + o
        o = acc / den
        out = o if sub == 0 else jnp.where(lo, out, o)
    return out


def _ctx_attn_kernel(q_ref, k_ref, v_ref, x_ref, mod_ref, wout_ref, o_ref):
    outs = []
    for hp in range(HEADS // 2):
        sl = slice(hp * 128, (hp + 1) * 128)
        outs.append(_head_pair_attention(
            q_ref[:, sl], [(k_ref[:, sl], v_ref[:, sl], lambda sub: None)]))
    o = jnp.concatenate(outs, axis=1).astype(BF16)
    o_ref[...] = x_ref[...] + mod_ref[2:3, :] * _dot(o, wout_ref[...])


def _ctx_attn(q, k, v, x, mod_l, w_out):
    n = SEG // SEQ
    blk = pl.BlockSpec((SEQ, D), lambda b: (b, 0))
    return pl.pallas_call(
        _ctx_attn_kernel,
        grid=(n,),
        in_specs=[blk, blk, blk, blk,
                  pl.BlockSpec((None, 6, D), lambda b: (0, 0, 0)),
                  pl.BlockSpec((D, D), lambda b: (0, 0))],
        out_specs=blk,
        out_shape=jax.ShapeDtypeStruct((SEG, D), F32),
        compiler_params=_params("parallel"),
        name="ctx_attn",
    )(q, k, v, x, mod_l, w_out)


def _na_window_start(r):
    rows = SEG // GRID_W
    return jnp.clip(r - NA_ROWS // 2, 0, rows - NA_ROWS)


def _na_attn_kernel(q_ref, k_ref, v_ref, kc_ref, vc_ref, bias_ref, x_ref, mod_ref, wout_ref, o_ref):
    r = pl.program_id(1)
    start = pl.multiple_of(_na_window_start(r) * GRID_W, GRID_W)
    n_loc = NA_ROWS * GRID_W
    outs = []
    for hp in range(HEADS // 2):
        sl = slice(hp * 128, (hp + 1) * 128)
        kw = k_ref[pl.ds(start, n_loc), sl]
        vw = v_ref[pl.ds(start, n_loc), sl]
        outs.append(_head_pair_attention(
            q_ref[:, sl],
            [(kw, vw, lambda sub, hp=hp: bias_ref[2 * hp + sub]),
             (kc_ref[:, sl], vc_ref[:, sl], lambda sub: None)]))
    o = jnp.concatenate(outs, axis=1).astype(BF16)
    o_ref[...] = x_ref[...] + mod_ref[2:3, :] * _dot(o, wout_ref[...])


def _na_attn(q, k, v, kc, vc, bias, x, mod_l, w_out):
    rows = SEG // GRID_W
    nb = (T - SEG) // SEG
    qblk = pl.BlockSpec((GRID_W, D), lambda b, r: (rows + b * rows + r, 0))
    img = pl.BlockSpec((SEG, D), lambda b, r: (1 + b, 0))
    ctx = pl.BlockSpec((None, SEQ, D), lambda b, r: (b, 0, 0))

    def bias_map(b, r):
        return (_na_window_start(r) - r + NA_ROWS - 1, 0, 0, 0)

    return pl.pallas_call(
        _na_attn_kernel,
        grid=(nb, rows),
        in_specs=[qblk, img, img, ctx, ctx,
                  pl.BlockSpec((None, HEADS, GRID_W, NA_ROWS * GRID_W), bias_map),
                  qblk,
                  pl.BlockSpec((None, 6, D), lambda b, r: (1 + b, 0, 0)),
                  pl.BlockSpec((D, D), lambda b, r: (0, 0))],
        out_specs=pl.BlockSpec((GRID_W, D), lambda b, r: (b * rows + r, 0)),
        out_shape=jax.ShapeDtypeStruct((T - SEG, D), F32),
        compiler_params=_params("parallel", "arbitrary"),
        name="na_attn",
    )(q, k, v, kc, vc, bias, x, mod_l, w_out)


def _na_bias_table(rpb):
    qc = np.arange(GRID_W)
    c0 = np.clip(qc - NA_COLS // 2, 0, GRID_W - NA_COLS)
    kc = np.arange(GRID_W)
    inside = (kc[None, :] >= c0[:, None]) & (kc[None, :] < c0[:, None] + NA_COLS)
    dc = kc[None, :] - qc[:, None] + NA_COLS - 1
    cols = jnp.zeros(rpb.shape[:2] + (GRID_W, GRID_W), F32)
    for d in range(2 * NA_COLS - 1):
        cols = jnp.where(jnp.asarray(inside & (dc == d))[None, None], rpb[:, :, d, None, None], cols)
    cols = jnp.where(jnp.asarray(inside)[None, None], cols, NEG)
    tab = jnp.stack([cols[:, var:var + NA_ROWS] for var in range(NA_ROWS)], axis=0)
    tab = jnp.transpose(tab, (0, 1, 3, 2, 4))
    return tab.reshape(NA_ROWS, HEADS, GRID_W, NA_ROWS * GRID_W)


def _peer_select_kernel(x_ref, mod_ref, n2_ref, wqt_ref, sk_ref, h_o, idx_o, g_o, q_scr):
    x = x_ref[...]
    h = _rms_mod(x, n2_ref[...], mod_ref[4:5, :], mod_ref[3:4, :])
    h_o[...] = h
    q_scr[...] = _dot_nt(wqt_ref[...], h.astype(BF16))

    n_c = P_TOPK * P_TOPK
    key_pos = lax.broadcasted_iota(jnp.int32, (P_KEYS, TM), 0).astype(F32)
    rank = lax.broadcasted_iota(jnp.int32, (P_TOPK, TM), 0)
    rank8 = lax.broadcasted_iota(jnp.int32, (8, TM), 0)
    cand_pos = jnp.concatenate(
        [rank] + [a * P_TOPK + rank8 for a in range(1, 8)] + [(8 + rank8) * P_TOPK], axis=0).astype(F32)
    neg_inf = jnp.float32(-jnp.inf)

    def head(hd, carry):
        top_val, top_idx = [], []
        for p in range(2):
            off = pl.multiple_of(hd * 2 * P_HALF + p * P_HALF, P_HALF)
            s = _dot(sk_ref[p], q_scr[pl.ds(off, P_HALF), :].astype(BF16))
            vals, idxs = [], []
            for it in range(P_TOPK):
                m = jnp.max(s, axis=0, keepdims=True)
                am = jnp.min(jnp.where(s == m, key_pos, float(P_KEYS)), axis=0, keepdims=True)
                s = jnp.where(key_pos == am, neg_inf, s)
                vals.append(m)
                idxs.append(am)
            top_val.append(vals)
            top_idx.append(idxs)
        s2 = jnp.zeros((P_TOPK, TM), F32)
        i2 = jnp.zeros((P_TOPK, TM), F32)
        for it in range(P_TOPK):
            s2 = jnp.where(rank == it, top_val[1][it], s2)
            i2 = jnp.where(rank == it, top_idx[1][it], i2)
        s1_hi = jnp.zeros((8, TM), F32)
        i1_hi = jnp.zeros((8, TM), F32)
        for it in range(8, P_TOPK):
            s1_hi = jnp.where(rank8 == it - 8, top_val[0][it], s1_hi)
            i1_hi = jnp.where(rank8 == it - 8, top_idx[0][it], i1_hi)
        cs_parts = [top_val[0][0] + s2]
        ce_parts = [top_idx[0][0] * float(P_KEYS) + i2]
        for a in range(1, 8):
            live = rank8 < P_TOPK // (a + 1)
            cs_parts.append(jnp.where(live, top_val[0][a] + s2[:8], neg_inf))
            ce_parts.append(top_idx[0][a] * float(P_KEYS) + i2[:8])
        cs_parts.append(s1_hi + top_val[1][0])
        ce_parts.append(i1_hi * float(P_KEYS) + top_idx[1][0])
        cs = jnp.concatenate(cs_parts, axis=0)
        ce = jnp.concatenate(ce_parts, axis=0)
        sc = jnp.zeros((P_TOPK, TM), F32)
        ex_id = jnp.zeros((P_TOPK, TM), F32)
        best = None
        for it in range(P_TOPK):
            m = jnp.max(cs, axis=0, keepdims=True)
            am = jnp.min(jnp.where(cs == m, cand_pos, float(n_c)), axis=0, keepdims=True)
            hit = cand_pos == am
            e = jnp.max(jnp.where(hit, ce, -1.0), axis=0, keepdims=True)
            cs = jnp.where(hit, neg_inf, cs)
            sc = jnp.where(rank == it, m, sc)
            ex_id = jnp.where(rank == it, e, ex_id)
            if it == 0:
                best = m
        ex = jnp.exp(sc - best)
        rows = pl.ds(pl.multiple_of(hd * P_TOPK, P_TOPK), P_TOPK)
        idx_o[rows, :] = ex_id.astype(jnp.int32)
        g_o[rows, :] = ex / jnp.sum(ex, axis=0, keepdims=True)
        return carry

    lax.fori_loop(0, P_HEADS, head, 0)


def _peer_select(x, mod_l, n2, wqt, sk):
    const2 = lambda i: (0, 0)
    blk = pl.BlockSpec((TM, D), lambda i: (i, 0))
    tblk = pl.BlockSpec((P_PAIRS, TM), lambda i: (0, i))
    return pl.pallas_call(
        _peer_select_kernel,
        grid=(T // TM,),
        in_specs=[
            blk,
            pl.BlockSpec((None, 6, D), lambda i: ((i * TM) // SEG, 0, 0)),
            pl.BlockSpec((1, D), const2),
            pl.BlockSpec((P_HEADS * 2 * P_HALF, D), const2),
            pl.BlockSpec((2, P_KEYS, P_HALF), lambda i: (0, 0, 0)),
        ],
        out_specs=[blk, tblk, tblk],
        out_shape=[jax.ShapeDtypeStruct((T, D), F32),
                   jax.ShapeDtypeStruct((P_PAIRS, T), jnp.int32),
                   jax.ShapeDtypeStruct((P_PAIRS, T), F32)],
        scratch_shapes=[pltpu.VMEM((P_HEADS * 2 * P_HALF, TM), F32)],
        compiler_params=_params("parallel"),
        name="peer_select",
    )(x, mod_l, n2, wqt, sk)


PEER_LANES = 2 * P_HEADS * P_PAIRS


def _peer_expert_kernel(idx_ref, h_ref, g_ref, x_ref, gate_ref, e2_ref, tab_ref, o_ref,
                        buf0, buf1, grep, sem):
    step = pl.program_id(0)
    n_batches = PEER_TB // PEER_G
    bufs = (buf0, buf1)

    def gather_wait(slot):
        pltpu.make_async_copy(tab_ref.at[pl.ds(0, PEER_ROWS)], bufs[slot], sem.at[slot]).wait()

    def gather_start(tok, j, slot, k0, k1):
        for k in range(k0, k1):
            pltpu.make_async_copy(
                tab_ref.at[idx_ref[tok, k]],
                bufs[slot].at[j * P_PAIRS + k],
                sem.at[slot]).start(priority=k % 2)

    @pl.when(step == 0)
    def _():
        for j in range(PEER_G):
            gather_start(j, j, 0, 0, P_PAIRS)

    grep[...] = _dot_f32_exact_rhs(g_ref[...], e2_ref[...])

    sub = lax.broadcasted_iota(jnp.int32, (8, PEER_LANES), 0)
    lane = lax.broadcasted_iota(jnp.int32, (8, PEER_LANES), 1)
    is_u = (lane % 16) == sub
    is_v = (lane % 16) == sub + 8

    def token_rows(slot, j):
        tiles = bufs[slot][j * P_PAIRS:(j + 1) * P_PAIRS]
        return pltpu.bitcast(tiles.reshape(P_PAIRS * 8, 128), BF16)

    def batch(b, slot):
        gather_wait(slot)
        t0 = pl.multiple_of(b * PEER_G, PEER_G)
        nxt = t0 + PEER_G
        part = jnp.zeros((PEER_G, PEER_LANES), F32)
        for j in range(PEER_G):
            gather_start(nxt + j, j, 1 - slot, 0, P_PAIRS // 2)
            hm = h_ref[pl.ds(pl.multiple_of((t0 + j) * 8, 8), 8), :].astype(BF16)
            r = _dot_nt(hm, token_rows(slot, j))
            c = jnp.sum(jnp.where(is_u, r, 0.0), axis=0, keepdims=True)
            part = jnp.where(sub == j, c, part)
        a = part
        for k in (1, 2, 4, 8):
            up = pltpu.roll(a, k, 1)
            down = pltpu.roll(a, PEER_LANES - k, 1)
            a = a + jnp.where((lane & k) != 0, up, down)
        w = jax.nn.gelu(a) * grep[pl.ds(t0, PEER_G), :]
        for j in range(PEER_G):
            gather_start(nxt + j, j, 1 - slot, P_PAIRS // 2, P_PAIRS)
            lj = jnp.where(is_v, jnp.broadcast_to(w[j:j + 1, :], (8, PEER_LANES)), 0.0)
            y = _dot(lj.astype(BF16), token_rows(slot, j))
            rows = pl.ds(pl.multiple_of((t0 + j) * 8, 8), 8)
            o_ref[rows, :] = x_ref[rows, :] + gate_ref[...] * y

    def batch_pair(i, carry):
        batch(2 * i, 0)
        batch(2 * i + 1, 1)
        return carry

    lax.fori_loop(0, n_batches // 2, batch_pair, 0)

    @pl.when(step == pl.num_programs(0) - 1)
    def _():
        gather_wait(0)


def _peer_expert(idx, h_tiles, g, x_tiles, gate_tiles, e2, table):
    n_steps = T // PEER_TB
    idx = idx.reshape(n_steps, PEER_TB, P_PAIRS)
    idx = jnp.concatenate([idx, jnp.roll(idx[:, :PEER_G], -1, axis=0)], axis=1)
    blk = pl.BlockSpec((PEER_TB * 8, 128), lambda i: (i, 0))
    return pl.pallas_call(
        _peer_expert_kernel,
        grid=(n_steps,),
        in_specs=[
            pl.BlockSpec((None, PEER_TB + PEER_G, P_PAIRS), lambda i: (i, 0, 0), memory_space=pltpu.SMEM),
            blk,
            pl.BlockSpec((PEER_TB, P_PAIRS), lambda i: (i, 0)),
            blk,
            pl.BlockSpec((None, 8, 128), lambda i: ((i * PEER_TB) // SEG, 0, 0)),
            pl.BlockSpec((P_PAIRS, PEER_LANES), lambda i: (0, 0)),
            pl.BlockSpec(memory_space=pl.ANY),
        ],
        out_specs=blk,
        out_shape=jax.ShapeDtypeStruct((T * 8, 128), F32),
        scratch_shapes=[pltpu.VMEM((PEER_ROWS, 8, 128), jnp.uint32),
                        pltpu.VMEM((PEER_ROWS, 8, 128), jnp.uint32),
                        pltpu.VMEM((PEER_TB, PEER_LANES), F32),
                        pltpu.SemaphoreType.DMA((2,))],
        compiler_params=_params("arbitrary"),
        name="peer_expert",
    )(idx, h_tiles, g, x_tiles, gate_tiles, e2, table)


def _pack_expert_table(u, v):
    def tiles(t):
        bits = lax.bitcast_convert_type(t.astype(BF16), jnp.uint16).astype(jnp.uint32)
        bits = bits.reshape(N_EXPERTS, 4, 2, 128)
        return bits[:, :, 0, :] | (bits[:, :, 1, :] << 16)
    return jnp.concatenate([tiles(u), tiles(v)], axis=1)


def kernel(x_prompt, x_sample, cache_k, cache_v, c, c_ctx, ada_w, ada_b, norm1_w, norm2_w,
           a_w_in, a_w_s, a_b_s, a_v_norm, a_w_out, b_w_qkv, b_q_norm, b_k_norm, b_rpb,
           b_w_out, peer_w_q, peer_sub_keys, peer_u, peer_v):
    n_prompt = x_prompt.shape[0]
    n_sample = x_sample.shape[0]
    x = jnp.concatenate([x_prompt.reshape(-1, D), x_sample.reshape(-1, D)], axis=0)

    cond8 = jnp.zeros((8, D), F32).at[0].set(c_ctx).at[1:1 + n_sample].set(c)
    mod = _ada_mod(cond8, ada_w, ada_b)

    head_of = jnp.arange(D) // HEAD_DIM
    ind = (head_of[:, None] == jnp.arange(128)[None, :]).astype(BF16)
    indt = ind.T
    pair_expand = (jnp.arange(P_PAIRS)[:, None] == jnp.arange(PEER_LANES)[None, :] // 16).astype(BF16)

    new_k, new_v = [], []
    for i in range(DEPTH):
        j = i // 2
        mod_l = mod[i]
        n1 = norm1_w[i].reshape(1, D)
        if i % 2 == 0:
            bexp = jnp.repeat(a_b_s[j].T, D // A_GROUPS, axis=1)
            x = _mix_a(x, mod_l, n1, a_w_in[j].astype(BF16), a_w_s[j].astype(BF16), bexp,
                       a_v_norm[j].reshape(1, D), a_w_out[j].astype(BF16))
        else:
            qn = jnp.tile(b_q_norm[j], HEADS).reshape(1, D)
            kn = jnp.tile(b_k_norm[j], HEADS).reshape(1, D)
            q, k, v, kf, vf = _qkv(x, mod_l, n1, b_w_qkv[j].astype(BF16), qn, kn, ind, indt)
            new_k.append(kf[:SEG].reshape(n_prompt, SEQ, HEADS, HEAD_DIM))
            new_v.append(vf[:SEG].reshape(n_prompt, SEQ, HEADS, HEAD_DIM))
            w_out = b_w_out[j].astype(BF16)
            x_ctx = _ctx_attn(q, k, v, x, mod_l, w_out)
            kc = cache_k[:, j].reshape(n_sample, SEQ, D).astype(BF16)
            vc = cache_v[:, j].reshape(n_sample, SEQ, D).astype(BF16)
            x_lat = _na_attn(q, k, v, kc, vc, _na_bias_table(b_rpb[j]), x, mod_l, w_out)
            x = jnp.concatenate([x_ctx, x_lat], axis=0)
        h, idx_t, g_t = _peer_select(x, mod_l, norm2_w[i].reshape(1, D), peer_w_q[i].T.astype(BF16),
                                     peer_sub_keys[i].astype(BF16))
        x = _peer_expert(idx_t.T, h.reshape(T * 8, 128), g_t.T, x.reshape(T * 8, 128),
                         mod_l[:, 5].reshape(8, 8, 128), pair_expand,
                         _pack_expert_table(peer_u[i], peer_v[i])).reshape(T, D)

    y_prompt = x[:SEG].reshape(x_prompt.shape)
    y_sample = x[SEG:].reshape(x_sample.shape)
    return (y_prompt, y_sample, jnp.stack(new_k, axis=1), jnp.stack(new_v, axis=1))
```

```python
import functools

import jax
import jax.numpy as jnp
import numpy as np
from jax import lax
from jax.experimental import pallas as pl
from jax.experimental.pallas import tpu as pltpu

F32 = jnp.float32
BF16 = jnp.bfloat16

D = 1024
DEPTH = 4
SEG = 4096
N_SEG = 3
T = N_SEG * SEG
SEQ = 256
GRID_W = 64
CHUNK = 128
A_GROUPS = 8
HEADS = 16
HEAD_DIM = 64
NA_ROWS = 8
NA_COLS = 16
P_HEADS = 8
P_HALF = 128
P_KEYS = 128
P_TOPK = 16
P_PAIRS = P_HEADS * P_TOPK
N_EXPERTS = P_KEYS * P_KEYS
EPS = 1e-6
NEG = -1e30

VMEM_LIMIT = 56 * 1024 * 1024

TM = 256
PEER_TB = 64
PEER_G = 8
PEER_ROWS = PEER_G * P_PAIRS


def _params(*sem):
    return pltpu.CompilerParams(dimension_semantics=sem, vmem_limit_bytes=VMEM_LIMIT)


def _dot(a, b):
    return jnp.dot(a, b, preferred_element_type=F32)


def _dot_nt(a, b):
    return lax.dot_general(a, b, (((1,), (1,)), ((), ())), preferred_element_type=F32)


def _split(a):
    hi = a.astype(BF16)
    lo = (a - hi.astype(F32)).astype(BF16)
    return hi, lo


def _dot_f32(a, b):
    ah, al = _split(a)
    bh, bl = _split(b)
    return _dot(ah, bh) + _dot(ah, bl) + _dot(al, bh)


def _dot_f32_exact_rhs(a, b_bf16):
    ah, al = _split(a)
    return _dot(ah, b_bf16) + _dot(al, b_bf16)


def _rms_mod(x, norm_w, scale, shift):
    y = x * lax.rsqrt(jnp.mean(x * x, axis=-1, keepdims=True) + EPS)
    return (y * norm_w) * (1.0 + scale) + shift


def _seg_of_block(tm):
    return lambda i: (i * tm) // SEG


def _ada_kernel(cond_ref, w_ref, b_ref, o_ref):
    c = cond_ref[...]
    s = c * jax.nn.sigmoid(c)
    o_ref[...] = _dot_f32(s, w_ref[...]) + b_ref[...]


def _ada_mod(cond8, ada_w, ada_b):
    tn = 1024
    n = 6 * D
    out = pl.pallas_call(
        _ada_kernel,
        grid=(DEPTH, n // tn),
        in_specs=[
            pl.BlockSpec((8, D), lambda l, j: (0, 0)),
            pl.BlockSpec((None, D, tn), lambda l, j: (l, 0, j)),
            pl.BlockSpec((None, 1, tn), lambda l, j: (l, 0, j)),
        ],
        out_specs=pl.BlockSpec((None, 8, tn), lambda l, j: (l, 0, j)),
        out_shape=jax.ShapeDtypeStruct((DEPTH, 8, n), F32),
        compiler_params=_params("parallel", "parallel"),
        name="ada_mod",
    )(cond8, ada_w, ada_b.reshape(DEPTH, 1, n))
    return out.reshape(DEPTH, 8, 6, D)


def _mix_a_kernel(x_ref, mod_ref, n1_ref, win_ref, ws_ref, bexp_ref, vn_ref, wout_ref, o_ref):
    x = x_ref[...]
    h = _rms_mod(x, n1_ref[...], mod_ref[1:2, :], mod_ref[0:1, :])
    z = jax.nn.gelu(_dot(h.astype(BF16), win_ref[...]))
    u = z[:, :D]
    v = z[:, D:]
    v = v * lax.rsqrt(jnp.mean(v * v, axis=-1, keepdims=True) + EPS) * vn_ref[...]
    vb = v.astype(BF16)
    rows = []
    for c in range(TM // CHUNK):
        cols = []
        for g in range(A_GROUPS):
            vg = vb[c * CHUNK:(c + 1) * CHUNK, g * 128:(g + 1) * 128]
            cols.append(_dot(ws_ref[g], vg))
        rows.append(jnp.concatenate(cols, axis=1) + bexp_ref[...])
    sv = jnp.concatenate(rows, axis=0)
    mix = _dot((u * sv).astype(BF16), wout_ref[...])
    o_ref[...] = x + mod_ref[2:3, :] * mix


def _mix_a(x, mod_l, n1, w_in, w_s, bexp, vn, w_out):
    const2 = lambda i: (0, 0)
    return pl.pallas_call(
        _mix_a_kernel,
        grid=(T // TM,),
        in_specs=[
            pl.BlockSpec((TM, D), lambda i: (i, 0)),
            pl.BlockSpec((None, 6, D), lambda i: ((i * TM) // SEG, 0, 0)),
            pl.BlockSpec((1, D), const2),
            pl.BlockSpec((D, 2 * D), const2),
            pl.BlockSpec((A_GROUPS, CHUNK, CHUNK), lambda i: (0, 0, 0)),
            pl.BlockSpec((CHUNK, D), const2),
            pl.BlockSpec((1, D), const2),
            pl.BlockSpec((D, D), const2),
        ],
        out_specs=pl.BlockSpec((TM, D), lambda i: (i, 0)),
        out_shape=jax.ShapeDtypeStruct((T, D), F32),
        compiler_params=_params("parallel"),
        name="mix_a",
    )(x, mod_l, n1, w_in, w_s, bexp, vn, w_out)


def _qkv_kernel(x_ref, mod_ref, n1_ref, w_ref, qn_ref, kn_ref, ind_ref, indt_ref,
                q_o, k_o, v_o, kf_o, vf_o):
    x = x_ref[...]
    h = _rms_mod(x, n1_ref[...], mod_ref[1:2, :], mod_ref[0:1, :])
    qkv = _dot(h.astype(BF16), w_ref[...])

    def head_norm(t, w):
        ss = _dot_f32_exact_rhs(t * t, ind_ref[...])
        inv = lax.rsqrt(ss * (1.0 / HEAD_DIM) + EPS)
        return t * _dot_f32_exact_rhs(inv, indt_ref[...]) * w

    q = head_norm(qkv[:, :D], qn_ref[...])
    k = head_norm(qkv[:, D:2 * D], kn_ref[...])
    v = qkv[:, 2 * D:]
    q_o[...] = (q * (HEAD_DIM ** -0.5)).astype(BF16)
    k_o[...] = k.astype(BF16)
    v_o[...] = v.astype(BF16)
    kf_o[...] = k
    vf_o[...] = v


def _qkv(x, mod_l, n1, w_qkv, qn, kn, ind, indt):
    const2 = lambda i: (0, 0)
    blk = pl.BlockSpec((TM, D), lambda i: (i, 0))
    return pl.pallas_call(
        _qkv_kernel,
        grid=(T // TM,),
        in_specs=[
            blk,
            pl.BlockSpec((None, 6, D), lambda i: ((i * TM) // SEG, 0, 0)),
            pl.BlockSpec((1, D), const2),
            pl.BlockSpec((D, 3 * D), const2),
            pl.BlockSpec((1, D), const2),
            pl.BlockSpec((1, D), const2),
            pl.BlockSpec((D, 128), const2),
            pl.BlockSpec((128, D), const2),
        ],
        out_specs=[blk, blk, blk, blk, blk],
        out_shape=[jax.ShapeDtypeStruct((T, D), BF16)] * 3 + [jax.ShapeDtypeStruct((T, D), F32)] * 2,
        compiler_params=_params("parallel"),
        name="qkv",
    )(x, mod_l, n1, w_qkv, qn, kn, ind, indt)


def _head_pair_attention(q2, score_parts):
    lane = lax.broadcasted_iota(jnp.int32, (1, 128), 1)
    lo = lane < HEAD_DIM
    out = None
    for sub in range(2):
        keep = lo if sub == 0 else jnp.logical_not(lo)
        qm = jnp.where(keep, q2, jnp.zeros_like(q2))
        scores = []
        for k2, _, bias_fn in score_parts:
            s = _dot_nt(qm, k2)
            b = bias_fn(sub)
            scores.append(s if b is None else s + b)
        m = scores[0].max(axis=-1, keepdims=True)
        for s in scores[1:]:
            m = jnp.maximum(m, s.max(axis=-1, keepdims=True))
        den = None
        acc = None
        for s, (_, v2, _) in zip(scores, score_parts):
            p = jnp.exp(s - m)
            d = p.sum(axis=-1, keepdims=True)
            o = _dot(p.astype(BF16), v2)
            den = d if den is None else den + d
            acc = o if acc is None else acc + o
        o = acc / den
        out = o if sub == 0 else jnp.where(lo, out, o)
    return out


def _ctx_attn_kernel(q_ref, k_ref, v_ref, x_ref, mod_ref, wout_ref, o_ref):
    outs = []
    for hp in range(HEADS // 2):
        sl = slice(hp * 128, (hp + 1) * 128)
        outs.append(_head_pair_attention(
            q_ref[:, sl], [(k_ref[:, sl], v_ref[:, sl], lambda sub: None)]))
    o = jnp.concatenate(outs, axis=1).astype(BF16)
    o_ref[...] = x_ref[...] + mod_ref[2:3, :] * _dot(o, wout_ref[...])


def _ctx_attn(q, k, v, x, mod_l, w_out):
    n = SEG // SEQ
    blk = pl.BlockSpec((SEQ, D), lambda b: (b, 0))
    return pl.pallas_call(
        _ctx_attn_kernel,
        grid=(n,),
        in_specs=[blk, blk, blk, blk,
                  pl.BlockSpec((None, 6, D), lambda b: (0, 0, 0)),
                  pl.BlockSpec((D, D), lambda b: (0, 0))],
        out_specs=blk,
        out_shape=jax.ShapeDtypeStruct((SEG, D), F32),
        compiler_params=_params("parallel"),
        name="ctx_attn",
    )(q, k, v, x, mod_l, w_out)


def _na_window_start(r):
    rows = SEG // GRID_W
    return jnp.clip(r - NA_ROWS // 2, 0, rows - NA_ROWS)


def _na_attn_kernel(q_ref, k_ref, v_ref, kc_ref, vc_ref, bias_ref, x_ref, mod_ref, wout_ref, o_ref):
    r = pl.program_id(1)
    start = pl.multiple_of(_na_window_start(r) * GRID_W, GRID_W)
    n_loc = NA_ROWS * GRID_W
    outs = []
    for hp in range(HEADS // 2):
        sl = slice(hp * 128, (hp + 1) * 128)
        kw = k_ref[pl.ds(start, n_loc), sl]
        vw = v_ref[pl.ds(start, n_loc), sl]
        outs.append(_head_pair_attention(
            q_ref[:, sl],
            [(kw, vw, lambda sub, hp=hp: bias_ref[2 * hp + sub]),
             (kc_ref[:, sl], vc_ref[:, sl], lambda sub: None)]))
    o = jnp.concatenate(outs, axis=1).astype(BF16)
    o_ref[...] = x_ref[...] + mod_ref[2:3, :] * _dot(o, wout_ref[...])


def _na_attn(q, k, v, kc, vc, bias, x, mod_l, w_out):
    rows = SEG // GRID_W
    nb = (T - SEG) // SEG
    qblk = pl.BlockSpec((GRID_W, D), lambda b, r: (rows + b * rows + r, 0))
    img = pl.BlockSpec((SEG, D), lambda b, r: (1 + b, 0))
    ctx = pl.BlockSpec((None, SEQ, D), lambda b, r: (b, 0, 0))

    def bias_map(b, r):
        return (_na_window_start(r) - r + NA_ROWS - 1, 0, 0, 0)

    return pl.pallas_call(
        _na_attn_kernel,
        grid=(nb, rows),
        in_specs=[qblk, img, img, ctx, ctx,
                  pl.BlockSpec((None, HEADS, GRID_W, NA_ROWS * GRID_W), bias_map),
                  qblk,
                  pl.BlockSpec((None, 6, D), lambda b, r: (1 + b, 0, 0)),
                  pl.BlockSpec((D, D), lambda b, r: (0, 0))],
        out_specs=pl.BlockSpec((GRID_W, D), lambda b, r: (b * rows + r, 0)),
        out_shape=jax.ShapeDtypeStruct((T - SEG, D), F32),
        compiler_params=_params("parallel", "arbitrary"),
        name="na_attn",
    )(q, k, v, kc, vc, bias, x, mod_l, w_out)


def _na_bias_table(rpb):
    qc = np.arange(GRID_W)
    c0 = np.clip(qc - NA_COLS // 2, 0, GRID_W - NA_COLS)
    kc = np.arange(GRID_W)
    inside = (kc[None, :] >= c0[:, None]) & (kc[None, :] < c0[:, None] + NA_COLS)
    dc = kc[None, :] - qc[:, None] + NA_COLS - 1
    cols = jnp.zeros(rpb.shape[:2] + (GRID_W, GRID_W), F32)
    for d in range(2 * NA_COLS - 1):
        cols = jnp.where(jnp.asarray(inside & (dc == d))[None, None], rpb[:, :, d, None, None], cols)
    cols = jnp.where(jnp.asarray(inside)[None, None], cols, NEG)
    tab = jnp.stack([cols[:, var:var + NA_ROWS] for var in range(NA_ROWS)], axis=0)
    tab = jnp.transpose(tab, (0, 1, 3, 2, 4))
    return tab.reshape(NA_ROWS, HEADS, GRID_W, NA_ROWS * GRID_W)


def _peer_select_kernel(x_ref, mod_ref, n2_ref, wqt_ref, sk_ref, h_o, idx_o, g_o, q_scr):
    x = x_ref[...]
    h = _rms_mod(x, n2_ref[...], mod_ref[4:5, :], mod_ref[3:4, :])
    h_o[...] = h
    q_scr[...] = _dot_nt(wqt_ref[...], h.astype(BF16))

    n_c = P_TOPK * P_TOPK
    key_pos = lax.broadcasted_iota(jnp.int32, (P_KEYS, TM), 0).astype(F32)
    rank = lax.broadcasted_iota(jnp.int32, (P_TOPK, TM), 0)
    rank8 = lax.broadcasted_iota(jnp.int32, (8, TM), 0)
    cand_pos = jnp.concatenate(
        [rank] + [a * P_TOPK + rank8 for a in range(1, 8)] + [(8 + rank8) * P_TOPK], axis=0).astype(F32)
    neg_inf = jnp.float32(-jnp.inf)

    def head(hd, carry):
        top_val, top_idx = [], []
        for p in range(2):
            off = pl.multiple_of(hd * 2 * P_HALF + p * P_HALF, P_HALF)
            s = _dot(sk_ref[p], q_scr[pl.ds(off, P_HALF), :].astype(BF16))
            vals, idxs = [], []
            for it in range(P_TOPK):
                m = jnp.max(s, axis=0, keepdims=True)
                am = jnp.min(jnp.where(s == m, key_pos, float(P_KEYS)), axis=0, keepdims=True)
                s = jnp.where(key_pos == am, neg_inf, s)
                vals.append(m)
                idxs.append(am)
            top_val.append(vals)
            top_idx.append(idxs)
        s2 = jnp.zeros((P_TOPK, TM), F32)
        i2 = jnp.zeros((P_TOPK, TM), F32)
        for it in range(P_TOPK):
            s2 = jnp.where(rank == it, top_val[1][it], s2)
            i2 = jnp.where(rank == it, top_idx[1][it], i2)
        s1_hi = jnp.zeros((8, TM), F32)
        i1_hi = jnp.zeros((8, TM), F32)
        for it in range(8, P_TOPK):
            s1_hi = jnp.where(rank8 == it - 8, top_val[0][it], s1_hi)
            i1_hi = jnp.where(rank8 == it - 8, top_idx[0][it], i1_hi)
        cs_parts = [top_val[0][0] + s2]
        ce_parts = [top_idx[0][0] * float(P_KEYS) + i2]
        for a in range(1, 8):
            live = rank8 < P_TOPK // (a + 1)
            cs_parts.append(jnp.where(live, top_val[0][a] + s2[:8], neg_inf))
            ce_parts.append(top_idx[0][a] * float(P_KEYS) + i2[:8])
        cs_parts.append(s1_hi + top_val[1][0])
        ce_parts.append(i1_hi * float(P_KEYS) + top_idx[1][0])
        cs = jnp.concatenate(cs_parts, axis=0)
        ce = jnp.concatenate(ce_parts, axis=0)
        sc = jnp.zeros((P_TOPK, TM), F32)
        ex_id = jnp.zeros((P_TOPK, TM), F32)
        best = None
        for it in range(P_TOPK):
            m = jnp.max(cs, axis=0, keepdims=True)
            am = jnp.min(jnp.where(cs == m, cand_pos, float(n_c)), axis=0, keepdims=True)
            hit = cand_pos == am
            e = jnp.max(jnp.where(hit, ce, -1.0), axis=0, keepdims=True)
            cs = jnp.where(hit, neg_inf, cs)
            sc = jnp.where(rank == it, m, sc)
            ex_id = jnp.where(rank == it, e, ex_id)
            if it == 0:
                best = m
        ex = jnp.exp(sc - best)
        rows = pl.ds(pl.multiple_of(hd * P_TOPK, P_TOPK), P_TOPK)
        idx_o[rows, :] = ex_id.astype(jnp.int32)
        g_o[rows, :] = ex / jnp.sum(ex, axis=0, keepdims=True)
        return carry

    lax.fori_loop(0, P_HEADS, head, 0)


def _peer_select(x, mod_l, n2, wqt, sk):
    const2 = lambda i: (0, 0)
    blk = pl.BlockSpec((TM, D), lambda i: (i, 0))
    tblk = pl.BlockSpec((P_PAIRS, TM), lambda i: (0, i))
    return pl.pallas_call(
        _peer_select_kernel,
        grid=(T // TM,),
        in_specs=[
            blk,
            pl.BlockSpec((None, 6, D), lambda i: ((i * TM) // SEG, 0, 0)),
            pl.BlockSpec((1, D), const2),
            pl.BlockSpec((P_HEADS * 2 * P_HALF, D), const2),
            pl.BlockSpec((2, P_KEYS, P_HALF), lambda i: (0, 0, 0)),
        ],
        out_specs=[blk, tblk, tblk],
        out_shape=[jax.ShapeDtypeStruct((T, D), F32),
                   jax.ShapeDtypeStruct((P_PAIRS, T), jnp.int32),
                   jax.ShapeDtypeStruct((P_PAIRS, T), F32)],
        scratch_shapes=[pltpu.VMEM((P_HEADS * 2 * P_HALF, TM), F32)],
        compiler_params=_params("parallel"),
        name="peer_select",
    )(x, mod_l, n2, wqt, sk)


PEER_LANES = 2 * P_HEADS * P_PAIRS


def _peer_expert_kernel(idx_ref, h_ref, g_ref, x_ref, gate_ref, e2_ref, tab_ref, o_ref,
                        buf0, buf1, grep, sem):
    step = pl.program_id(0)
    n_batches = PEER_TB // PEER_G
    bufs = (buf0, buf1)

    def gather_wait(slot, j):
        rows = pl.ds(j * P_PAIRS, P_PAIRS)
        pltpu.make_async_copy(tab_ref.at[rows], bufs[slot].at[rows], sem.at[slot, j]).wait()

    def gather_start(tok, j, slot):
        for k in range(P_PAIRS):
            pltpu.make_async_copy(
                tab_ref.at[idx_ref[tok, k]],
                bufs[slot].at[j * P_PAIRS + k],
                sem.at[slot, j]).start(priority=k % 2)

    @pl.when(step == 0)
    def _():
        for j in range(PEER_G):
            gather_start(j, j, 0)

    grep[...] = _dot_f32_exact_rhs(g_ref[...], e2_ref[...])

    sub = lax.broadcasted_iota(jnp.int32, (8, PEER_LANES), 0)
    lane = lax.broadcasted_iota(jnp.int32, (8, PEER_LANES), 1)
    is_u = (lane % 16) == sub
    is_v = (lane % 16) == sub + 8

    def token_rows(slot, j):
        tiles = bufs[slot][j * P_PAIRS:(j + 1) * P_PAIRS]
        return pltpu.bitcast(tiles.reshape(P_PAIRS * 8, 128), BF16)

    def batch(b, slot):
        t0 = pl.multiple_of(b * PEER_G, PEER_G)
        nxt = t0 + PEER_G
        part = jnp.zeros((PEER_G, PEER_LANES), F32)
        for j in range(PEER_G):
            gather_start(nxt + j, j, 1 - slot)
            gather_wait(slot, j)
            hm = h_ref[pl.ds(pl.multiple_of((t0 + j) * 8, 8), 8), :].astype(BF16)
            r = _dot_nt(hm, token_rows(slot, j))
            c = jnp.sum(jnp.where(is_u, r, 0.0), axis=0, keepdims=True)
            part = jnp.where(sub == j, c, part)
        a = part
        for k in (1, 2, 4, 8):
            up = pltpu.roll(a, k, 1)
            down = pltpu.roll(a, PEER_LANES - k, 1)
            a = a + jnp.where((lane & k) != 0, up, down)
        w = jax.nn.gelu(a) * grep[pl.ds(t0, PEER_G), :]
        for j in range(PEER_G):
            lj = jnp.where(is_v, jnp.broadcast_to(w[j:j + 1, :], (8, PEER_LANES)), 0.0)
            y = _dot(lj.astype(BF16), token_rows(slot, j))
            rows = pl.ds(pl.multiple_of((t0 + j) * 8, 8), 8)
            o_ref[rows, :] = x_ref[rows, :] + gate_ref[...] * y

    def batch_pair(i, carry):
        batch(2 * i, 0)
        batch(2 * i + 1, 1)
        return carry

    lax.fori_loop(0, n_batches // 2, batch_pair, 0)

    @pl.when(step == pl.num_programs(0) - 1)
    def _():
        for j in range(PEER_G):
            gather_wait(0, j)


def _peer_expert(idx, h_tiles, g, x_tiles, gate_tiles, e2, table):
    n_steps = T // PEER_TB
    idx = idx.reshape(n_steps, PEER_TB, P_PAIRS)
    idx = jnp.concatenate([idx, jnp.roll(idx[:, :PEER_G], -1, axis=0)], axis=1)
    blk = pl.BlockSpec((PEER_TB * 8, 128), lambda i: (i, 0))
    return pl.pallas_call(
        _peer_expert_kernel,
        grid=(n_steps,),
        in_specs=[
            pl.BlockSpec((None, PEER_TB + PEER_G, P_PAIRS), lambda i: (i, 0, 0), memory_space=pltpu.SMEM),
            blk,
            pl.BlockSpec((PEER_TB, P_PAIRS), lambda i: (i, 0)),
            blk,
            pl.BlockSpec((None, 8, 128), lambda i: ((i * PEER_TB) // SEG, 0, 0)),
            pl.BlockSpec((P_PAIRS, PEER_LANES), lambda i: (0, 0)),
            pl.BlockSpec(memory_space=pl.ANY),
        ],
        out_specs=blk,
        out_shape=jax.ShapeDtypeStruct((T * 8, 128), F32),
        scratch_shapes=[pltpu.VMEM((PEER_ROWS, 8, 128), jnp.uint32),
                        pltpu.VMEM((PEER_ROWS, 8, 128), jnp.uint32),
                        pltpu.VMEM((PEER_TB, PEER_LANES), F32),
                        pltpu.SemaphoreType.DMA((2, PEER_G))],
        compiler_params=_params("arbitrary"),
        name="peer_expert",
    )(idx, h_tiles, g, x_tiles, gate_tiles, e2, table)


def _pack_expert_table(u, v):
    def tiles(t):
        bits = lax.bitcast_convert_type(t.astype(BF16), jnp.uint16).astype(jnp.uint32)
        bits = bits.reshape(N_EXPERTS, 4, 2, 128)
        return bits[:, :, 0, :] | (bits[:, :, 1, :] << 16)
    return jnp.concatenate([tiles(u), tiles(v)], axis=1)


def kernel(x_prompt, x_sample, cache_k, cache_v, c, c_ctx, ada_w, ada_b, norm1_w, norm2_w,
           a_w_in, a_w_s, a_b_s, a_v_norm, a_w_out, b_w_qkv, b_q_norm, b_k_norm, b_rpb,
           b_w_out, peer_w_q, peer_sub_keys, peer_u, peer_v):
    n_prompt = x_prompt.shape[0]
    n_sample = x_sample.shape[0]
    x = jnp.concatenate([x_prompt.reshape(-1, D), x_sample.reshape(-1, D)], axis=0)

    cond8 = jnp.zeros((8, D), F32).at[0].set(c_ctx).at[1:1 + n_sample].set(c)
    mod = _ada_mod(cond8, ada_w, ada_b)

    head_of = jnp.arange(D) // HEAD_DIM
    ind = (head_of[:, None] == jnp.arange(128)[None, :]).astype(BF16)
    indt = ind.T
    pair_expand = (jnp.arange(P_PAIRS)[:, None] == jnp.arange(PEER_LANES)[None, :] // 16).astype(BF16)

    new_k, new_v = [], []
    for i in range(DEPTH):
        j = i // 2
        mod_l = mod[i]
        n1 = norm1_w[i].reshape(1, D)
        if i % 2 == 0:
            bexp = jnp.repeat(a_b_s[j].T, D // A_GROUPS, axis=1)
            x = _mix_a(x, mod_l, n1, a_w_in[j].astype(BF16), a_w_s[j].astype(BF16), bexp,
                       a_v_norm[j].reshape(1, D), a_w_out[j].astype(BF16))
        else:
            qn = jnp.tile(b_q_norm[j], HEADS).reshape(1, D)
            kn = jnp.tile(b_k_norm[j], HEADS).reshape(1, D)
            q, k, v, kf, vf = _qkv(x, mod_l, n1, b_w_qkv[j].astype(BF16), qn, kn, ind, indt)
            new_k.append(kf[:SEG].reshape(n_prompt, SEQ, HEADS, HEAD_DIM))
            new_v.append(vf[:SEG].reshape(n_prompt, SEQ, HEADS, HEAD_DIM))
            w_out = b_w_out[j].astype(BF16)
            x_ctx = _ctx_attn(q, k, v, x, mod_l, w_out)
            kc = cache_k[:, j].reshape(n_sample, SEQ, D).astype(BF16)
            vc = cache_v[:, j].reshape(n_sample, SEQ, D).astype(BF16)
            x_lat = _na_attn(q, k, v, kc, vc, _na_bias_table(b_rpb[j]), x, mod_l, w_out)
            x = jnp.concatenate([x_ctx, x_lat], axis=0)
        h, idx_t, g_t = _peer_select(x, mod_l, norm2_w[i].reshape(1, D), peer_w_q[i].T.astype(BF16),
                                     peer_sub_keys[i].astype(BF16))
        x = _peer_expert(idx_t.T, h.reshape(T * 8, 128), g_t.T, x.reshape(T * 8, 128),
                         mod_l[:, 5].reshape(8, 8, 128), pair_expand,
                         _pack_expert_table(peer_u[i], peer_v[i])).reshape(T, D)

    y_prompt = x[:SEG].reshape(x_prompt.shape)
    y_sample = x[SEG:].reshape(x_sample.shape)
    return (y_prompt, y_sample, jnp.stack(new_k, axis=1), jnp.stack(new_v, axis=1))
```

```python
import functools

import jax
import jax.numpy as jnp
import numpy as np
from jax import lax
from jax.experimental import pallas as pl
from jax.experimental.pallas import tpu as pltpu

F32 = jnp.float32
BF16 = jnp.bfloat16

D = 1024
DEPTH = 4
SEG = 4096
N_SEG = 3
T = N_SEG * SEG
SEQ = 256
GRID_W = 64
CHUNK = 128
A_GROUPS = 8
HEADS = 16
HEAD_DIM = 64
NA_ROWS = 8
NA_COLS = 16
P_HEADS = 8
P_HALF = 128
P_KEYS = 128
P_TOPK = 16
P_PAIRS = P_HEADS * P_TOPK
N_EXPERTS = P_KEYS * P_KEYS
EPS = 1e-6
NEG = -1e30

VMEM_LIMIT = 56 * 1024 * 1024

TM = 256
PEER_TB = 64
PEER_G = 8
PEER_ROWS = PEER_G * P_PAIRS


def _params(*sem):
    return pltpu.CompilerParams(dimension_semantics=sem, vmem_limit_bytes=VMEM_LIMIT)


def _dot(a, b):
    return jnp.dot(a, b, preferred_element_type=F32)


def _dot_nt(a, b):
    return lax.dot_general(a, b, (((1,), (1,)), ((), ())), preferred_element_type=F32)


def _split(a):
    hi = a.astype(BF16)
    lo = (a - hi.astype(F32)).astype(BF16)
    return hi, lo


def _dot_f32(a, b):
    ah, al = _split(a)
    bh, bl = _split(b)
    return _dot(ah, bh) + _dot(ah, bl) + _dot(al, bh)


def _dot_f32_exact_rhs(a, b_bf16):
    ah, al = _split(a)
    return _dot(ah, b_bf16) + _dot(al, b_bf16)


def _rms_mod(x, norm_w, scale, shift):
    y = x * lax.rsqrt(jnp.mean(x * x, axis=-1, keepdims=True) + EPS)
    return (y * norm_w) * (1.0 + scale) + shift


def _seg_of_block(tm):
    return lambda i: (i * tm) // SEG


def _ada_kernel(cond_ref, w_ref, b_ref, o_ref):
    c = cond_ref[...]
    s = c * jax.nn.sigmoid(c)
    o_ref[...] = _dot_f32(s, w_ref[...]) + b_ref[...]


def _ada_mod(cond8, ada_w, ada_b):
    tn = 1024
    n = 6 * D
    out = pl.pallas_call(
        _ada_kernel,
        grid=(DEPTH, n // tn),
        in_specs=[
            pl.BlockSpec((8, D), lambda l, j: (0, 0)),
            pl.BlockSpec((None, D, tn), lambda l, j: (l, 0, j)),
            pl.BlockSpec((None, 1, tn), lambda l, j: (l, 0, j)),
        ],
        out_specs=pl.BlockSpec((None, 8, tn), lambda l, j: (l, 0, j)),
        out_shape=jax.ShapeDtypeStruct((DEPTH, 8, n), F32),
        compiler_params=_params("parallel", "parallel"),
        name="ada_mod",
    )(cond8, ada_w, ada_b.reshape(DEPTH, 1, n))
    return out.reshape(DEPTH, 8, 6, D)


def _mix_a_kernel(x_ref, mod_ref, n1_ref, win_ref, ws_ref, bexp_ref, vn_ref, wout_ref, o_ref):
    x = x_ref[...]
    h = _rms_mod(x, n1_ref[...], mod_ref[1:2, :], mod_ref[0:1, :])
    z = jax.nn.gelu(_dot(h.astype(BF16), win_ref[...]))
    u = z[:, :D]
    v = z[:, D:]
    v = v * lax.rsqrt(jnp.mean(v * v, axis=-1, keepdims=True) + EPS) * vn_ref[...]
    vb = v.astype(BF16)
    rows = []
    for c in range(TM // CHUNK):
        cols = []
        for g in range(A_GROUPS):
            vg = vb[c * CHUNK:(c + 1) * CHUNK, g * 128:(g + 1) * 128]
            cols.append(_dot(ws_ref[g], vg))
        rows.append(jnp.concatenate(cols, axis=1) + bexp_ref[...])
    sv = jnp.concatenate(rows, axis=0)
    mix = _dot((u * sv).astype(BF16), wout_ref[...])
    o_ref[...] = x + mod_ref[2:3, :] * mix


def _mix_a(x, mod_l, n1, w_in, w_s, bexp, vn, w_out):
    const2 = lambda i: (0, 0)
    return pl.pallas_call(
        _mix_a_kernel,
        grid=(T // TM,),
        in_specs=[
            pl.BlockSpec((TM, D), lambda i: (i, 0)),
            pl.BlockSpec((None, 6, D), lambda i: ((i * TM) // SEG, 0, 0)),
            pl.BlockSpec((1, D), const2),
            pl.BlockSpec((D, 2 * D), const2),
            pl.BlockSpec((A_GROUPS, CHUNK, CHUNK), lambda i: (0, 0, 0)),
            pl.BlockSpec((CHUNK, D), const2),
            pl.BlockSpec((1, D), const2),
            pl.BlockSpec((D, D), const2),
        ],
        out_specs=pl.BlockSpec((TM, D), lambda i: (i, 0)),
        out_shape=jax.ShapeDtypeStruct((T, D), F32),
        compiler_params=_params("parallel"),
        name="mix_a",
    )(x, mod_l, n1, w_in, w_s, bexp, vn, w_out)


def _qkv_kernel(x_ref, mod_ref, n1_ref, w_ref, qn_ref, kn_ref, ind_ref, indt_ref,
                q_o, k_o, v_o, kf_o, vf_o):
    x = x_ref[...]
    h = _rms_mod(x, n1_ref[...], mod_ref[1:2, :], mod_ref[0:1, :])
    qkv = _dot(h.astype(BF16), w_ref[...])

    def head_norm(t, w):
        ss = _dot_f32_exact_rhs(t * t, ind_ref[...])
        inv = lax.rsqrt(ss * (1.0 / HEAD_DIM) + EPS)
        return t * _dot_f32_exact_rhs(inv, indt_ref[...]) * w

    q = head_norm(qkv[:, :D], qn_ref[...])
    k = head_norm(qkv[:, D:2 * D], kn_ref[...])
    v = qkv[:, 2 * D:]
    q_o[...] = (q * (HEAD_DIM ** -0.5)).astype(BF16)
    k_o[...] = k.astype(BF16)
    v_o[...] = v.astype(BF16)
    kf_o[...] = k
    vf_o[...] = v


def _qkv(x, mod_l, n1, w_qkv, qn, kn, ind, indt):
    const2 = lambda i: (0, 0)
    blk = pl.BlockSpec((TM, D), lambda i: (i, 0))
    return pl.pallas_call(
        _qkv_kernel,
        grid=(T // TM,),
        in_specs=[
            blk,
            pl.BlockSpec((None, 6, D), lambda i: ((i * TM) // SEG, 0, 0)),
            pl.BlockSpec((1, D), const2),
            pl.BlockSpec((D, 3 * D), const2),
            pl.BlockSpec((1, D), const2),
            pl.BlockSpec((1, D), const2),
            pl.BlockSpec((D, 128), const2),
            pl.BlockSpec((128, D), const2),
        ],
        out_specs=[blk, blk, blk, blk, blk],
        out_shape=[jax.ShapeDtypeStruct((T, D), BF16)] * 3 + [jax.ShapeDtypeStruct((T, D), F32)] * 2,
        compiler_params=_params("parallel"),
        name="qkv",
    )(x, mod_l, n1, w_qkv, qn, kn, ind, indt)


def _head_pair_attention(q2, score_parts):
    lane = lax.broadcasted_iota(jnp.int32, (1, 128), 1)
    lo = lane < HEAD_DIM
    out = None
    for sub in range(2):
        keep = lo if sub == 0 else jnp.logical_not(lo)
        qm = jnp.where(keep, q2, jnp.zeros_like(q2))
        scores = []
        for k2, _, bias_fn in score_parts:
            s = _dot_nt(qm, k2)
            b = bias_fn(sub)
            scores.append(s if b is None else s + b)
        m = scores[0].max(axis=-1, keepdims=True)
        for s in scores[1:]:
            m = jnp.maximum(m, s.max(axis=-1, keepdims=True))
        den = None
        acc = None
        for s, (_, v2, _) in zip(scores, score_parts):
            p = jnp.exp(s - m)
            d = p.sum(axis=-1, keepdims=True)
            o = _dot(p.astype(BF16), v2)
            den = d if den is None else den + d
            acc = o if acc is None else acc + o
        o = acc / den
        out = o if sub == 0 else jnp.where(lo, out, o)
    return out


def _ctx_attn_kernel(q_ref, k_ref, v_ref, x_ref, mod_ref, wout_ref, o_ref):
    outs = []
    for hp in range(HEADS // 2):
        sl = slice(hp * 128, (hp + 1) * 128)
        outs.append(_head_pair_attention(
            q_ref[:, sl], [(k_ref[:, sl], v_ref[:, sl], lambda sub: None)]))
    o = jnp.concatenate(outs, axis=1).astype(BF16)
    o_ref[...] = x_ref[...] + mod_ref[2:3, :] * _dot(o, wout_ref[...])


def _ctx_attn(q, k, v, x, mod_l, w_out):
    n = SEG // SEQ
    blk = pl.BlockSpec((SEQ, D), lambda b: (b, 0))
    return pl.pallas_call(
        _ctx_attn_kernel,
        grid=(n,),
        in_specs=[blk, blk, blk, blk,
                  pl.BlockSpec((None, 6, D), lambda b: (0, 0, 0)),
                  pl.BlockSpec((D, D), lambda b: (0, 0))],
        out_specs=blk,
        out_shape=jax.ShapeDtypeStruct((T, D), F32),
        input_output_aliases={3: 0},
        compiler_params=_params("parallel"),
        name="ctx_attn",
    )(q, k, v, x, mod_l, w_out)


def _na_window_start(r):
    rows = SEG // GRID_W
    return jnp.clip(r - NA_ROWS // 2, 0, rows - NA_ROWS)


def _na_attn_kernel(q_ref, k_ref, v_ref, kc_ref, vc_ref, bias_ref, x_ref, mod_ref, wout_ref, o_ref):
    r = pl.program_id(1)
    start = pl.multiple_of(_na_window_start(r) * GRID_W, GRID_W)
    n_loc = NA_ROWS * GRID_W
    outs = []
    for hp in range(HEADS // 2):
        sl = slice(hp * 128, (hp + 1) * 128)
        kw = k_ref[pl.ds(start, n_loc), sl]
        vw = v_ref[pl.ds(start, n_loc), sl]
        outs.append(_head_pair_attention(
            q_ref[:, sl],
            [(kw, vw, lambda sub, hp=hp: bias_ref[2 * hp + sub]),
             (kc_ref[:, sl], vc_ref[:, sl], lambda sub: None)]))
    o = jnp.concatenate(outs, axis=1).astype(BF16)
    o_ref[...] = x_ref[...] + mod_ref[2:3, :] * _dot(o, wout_ref[...])


def _na_attn(q, k, v, kc, vc, bias, x, mod_l, w_out):
    rows = SEG // GRID_W
    nb = (T - SEG) // SEG
    qblk = pl.BlockSpec((GRID_W, D), lambda b, r: (rows + b * rows + r, 0))
    img = pl.BlockSpec((SEG, D), lambda b, r: (1 + b, 0))
    ctx = pl.BlockSpec((None, SEQ, D), lambda b, r: (b, 0, 0))

    def bias_map(b, r):
        return (_na_window_start(r) - r + NA_ROWS - 1, 0, 0, 0)

    return pl.pallas_call(
        _na_attn_kernel,
        grid=(nb, rows),
        in_specs=[qblk, img, img, ctx, ctx,
                  pl.BlockSpec((None, HEADS, GRID_W, NA_ROWS * GRID_W), bias_map),
                  qblk,
                  pl.BlockSpec((None, 6, D), lambda b, r: (1 + b, 0, 0)),
                  pl.BlockSpec((D, D), lambda b, r: (0, 0))],
        out_specs=qblk,
        out_shape=jax.ShapeDtypeStruct((T, D), F32),
        input_output_aliases={6: 0},
        compiler_params=_params("parallel", "arbitrary"),
        name="na_attn",
    )(q, k, v, kc, vc, bias, x, mod_l, w_out)


def _na_bias_table(rpb):
    qc = np.arange(GRID_W)
    c0 = np.clip(qc - NA_COLS // 2, 0, GRID_W - NA_COLS)
    kc = np.arange(GRID_W)
    inside = (kc[None, :] >= c0[:, None]) & (kc[None, :] < c0[:, None] + NA_COLS)
    dc = kc[None, :] - qc[:, None] + NA_COLS - 1
    cols = jnp.zeros(rpb.shape[:2] + (GRID_W, GRID_W), F32)
    for d in range(2 * NA_COLS - 1):
        cols = jnp.where(jnp.asarray(inside & (dc == d))[None, None], rpb[:, :, d, None, None], cols)
    cols = jnp.where(jnp.asarray(inside)[None, None], cols, NEG)
    tab = jnp.stack([cols[:, var:var + NA_ROWS] for var in range(NA_ROWS)], axis=0)
    tab = jnp.transpose(tab, (0, 1, 3, 2, 4))
    return tab.reshape(NA_ROWS, HEADS, GRID_W, NA_ROWS * GRID_W)


def _peer_select_kernel(x_ref, mod_ref, n2_ref, wqt_ref, sk_ref, h_o, idx_o, g_o, q_scr):
    x = x_ref[...]
    h = _rms_mod(x, n2_ref[...], mod_ref[4:5, :], mod_ref[3:4, :])
    h_o[...] = h
    q_scr[...] = _dot_nt(wqt_ref[...], h.astype(BF16))

    n_c = P_TOPK * P_TOPK
    key_pos = lax.broadcasted_iota(jnp.int32, (P_KEYS, TM), 0).astype(F32)
    rank = lax.broadcasted_iota(jnp.int32, (P_TOPK, TM), 0)
    rank8 = lax.broadcasted_iota(jnp.int32, (8, TM), 0)
    cand_pos = jnp.concatenate(
        [rank] + [a * P_TOPK + rank8 for a in range(1, 8)] + [(8 + rank8) * P_TOPK], axis=0).astype(F32)
    neg_inf = jnp.float32(-jnp.inf)

    def head(hd, carry):
        top_val, top_idx = [], []
        for p in range(2):
            off = pl.multiple_of(hd * 2 * P_HALF + p * P_HALF, P_HALF)
            s = _dot(sk_ref[p], q_scr[pl.ds(off, P_HALF), :].astype(BF16))
            vals, idxs = [], []
            for it in range(P_TOPK):
                m = jnp.max(s, axis=0, keepdims=True)
                am = jnp.min(jnp.where(s == m, key_pos, float(P_KEYS)), axis=0, keepdims=True)
                s = jnp.where(key_pos == am, neg_inf, s)
                vals.append(m)
                idxs.append(am)
            top_val.append(vals)
            top_idx.append(idxs)
        s2 = jnp.zeros((P_TOPK, TM), F32)
        i2 = jnp.zeros((P_TOPK, TM), F32)
        for it in range(P_TOPK):
            s2 = jnp.where(rank == it, top_val[1][it], s2)
            i2 = jnp.where(rank == it, top_idx[1][it], i2)
        s1_hi = jnp.zeros((8, TM), F32)
        i1_hi = jnp.zeros((8, TM), F32)
        for it in range(8, P_TOPK):
            s1_hi = jnp.where(rank8 == it - 8, top_val[0][it], s1_hi)
            i1_hi = jnp.where(rank8 == it - 8, top_idx[0][it], i1_hi)
        cs_parts = [top_val[0][0] + s2]
        ce_parts = [top_idx[0][0] * float(P_KEYS) + i2]
        for a in range(1, 8):
            live = rank8 < P_TOPK // (a + 1)
            cs_parts.append(jnp.where(live, top_val[0][a] + s2[:8], neg_inf))
            ce_parts.append(top_idx[0][a] * float(P_KEYS) + i2[:8])
        cs_parts.append(s1_hi + top_val[1][0])
        ce_parts.append(i1_hi * float(P_KEYS) + top_idx[1][0])
        cs = jnp.concatenate(cs_parts, axis=0)
        ce = jnp.concatenate(ce_parts, axis=0)
        sc = jnp.zeros((P_TOPK, TM), F32)
        ex_id = jnp.zeros((P_TOPK, TM), F32)
        best = None
        for it in range(P_TOPK):
            m = jnp.max(cs, axis=0, keepdims=True)
            am = jnp.min(jnp.where(cs == m, cand_pos, float(n_c)), axis=0, keepdims=True)
            hit = cand_pos == am
            e = jnp.max(jnp.where(hit, ce, -1.0), axis=0, keepdims=True)
            cs = jnp.where(hit, neg_inf, cs)
            sc = jnp.where(rank == it, m, sc)
            ex_id = jnp.where(rank == it, e, ex_id)
            if it == 0:
                best = m
        ex = jnp.exp(sc - best)
        rows = pl.ds(pl.multiple_of(hd * P_TOPK, P_TOPK), P_TOPK)
        idx_o[rows, :] = ex_id.astype(jnp.int32)
        g_o[rows, :] = ex / jnp.sum(ex, axis=0, keepdims=True)
        return carry

    lax.fori_loop(0, P_HEADS, head, 0)


def _peer_select(x, mod_l, n2, wqt, sk):
    const2 = lambda i: (0, 0)
    blk = pl.BlockSpec((TM, D), lambda i: (i, 0))
    tblk = pl.BlockSpec((P_PAIRS, TM), lambda i: (0, i))
    return pl.pallas_call(
        _peer_select_kernel,
        grid=(T // TM,),
        in_specs=[
            blk,
            pl.BlockSpec((None, 6, D), lambda i: ((i * TM) // SEG, 0, 0)),
            pl.BlockSpec((1, D), const2),
            pl.BlockSpec((P_HEADS * 2 * P_HALF, D), const2),
            pl.BlockSpec((2, P_KEYS, P_HALF), lambda i: (0, 0, 0)),
        ],
        out_specs=[blk, tblk, tblk],
        out_shape=[jax.ShapeDtypeStruct((T, D), F32),
                   jax.ShapeDtypeStruct((P_PAIRS, T), jnp.int32),
                   jax.ShapeDtypeStruct((P_PAIRS, T), F32)],
        scratch_shapes=[pltpu.VMEM((P_HEADS * 2 * P_HALF, TM), F32)],
        compiler_params=_params("parallel"),
        name="peer_select",
    )(x, mod_l, n2, wqt, sk)


PEER_LANES = 2 * P_HEADS * P_PAIRS


def _peer_expert_kernel(idx_ref, h_ref, g_ref, x_ref, gate_ref, e2_ref, tab_ref, o_ref,
                        buf0, buf1, grep, sem):
    step = pl.program_id(0)
    n_batches = PEER_TB // PEER_G
    bufs = (buf0, buf1)

    def gather_wait(slot, j):
        rows = pl.ds(j * P_PAIRS, P_PAIRS)
        pltpu.make_async_copy(tab_ref.at[rows], bufs[slot].at[rows], sem.at[slot, j]).wait()

    def gather_start(tok, j, slot):
        for k in range(P_PAIRS):
            pltpu.make_async_copy(
                tab_ref.at[idx_ref[tok, k]],
                bufs[slot].at[j * P_PAIRS + k],
                sem.at[slot, j]).start(priority=k % 2)

    @pl.when(step == 0)
    def _():
        for j in range(PEER_G):
            gather_start(j, j, 0)

    grep[...] = _dot_f32_exact_rhs(g_ref[...], e2_ref[...])

    sub = lax.broadcasted_iota(jnp.int32, (8, PEER_LANES), 0)
    lane = lax.broadcasted_iota(jnp.int32, (8, PEER_LANES), 1)
    is_u = (lane % 16) == sub
    is_v = (lane % 16) == sub + 8

    def token_rows(slot, j):
        tiles = bufs[slot][j * P_PAIRS:(j + 1) * P_PAIRS]
        return pltpu.bitcast(tiles.reshape(P_PAIRS * 8, 128), BF16)

    def tile_rows(b, j):
        return pl.ds(pl.multiple_of((b * PEER_G + j) * 8, 8), 8)

    def combine(b, slot, j, w):
        lj = jnp.where(is_v, jnp.broadcast_to(w[j:j + 1, :], (8, PEER_LANES)), 0.0)
        y = _dot(lj.astype(BF16), token_rows(slot, j))
        o_ref[tile_rows(b, j), :] = x_ref[tile_rows(b, j), :] + gate_ref[...] * y

    def batch(b, slot):
        part = jnp.zeros((PEER_G, PEER_LANES), F32)
        for j in range(PEER_G):
            gather_start((b + 1) * PEER_G + j, j, 1 - slot)
            gather_wait(slot, j)
            r = _dot_nt(h_ref[tile_rows(b, j), :].astype(BF16), token_rows(slot, j))
            c = jnp.sum(jnp.where(is_u, r, 0.0), axis=0, keepdims=True)
            part = jnp.where(sub == j, c, part)
        a = part
        for k in (1, 2, 4, 8):
            up = pltpu.roll(a, k, 1)
            down = pltpu.roll(a, PEER_LANES - k, 1)
            a = a + jnp.where((lane & k) != 0, up, down)
        w = jax.nn.gelu(a) * grep[pl.ds(pl.multiple_of(b * PEER_G, PEER_G), PEER_G), :]
        for j in range(PEER_G):
            combine(b, slot, j, w)

    def batch_pair(i, carry):
        batch(2 * i, 0)
        batch(2 * i + 1, 1)
        return carry

    lax.fori_loop(0, n_batches // 2, batch_pair, 0)

    @pl.when(step == pl.num_programs(0) - 1)
    def _():
        for j in range(PEER_G):
            gather_wait(0, j)


def _peer_expert(idx, h_tiles, g, x_tiles, gate_tiles, e2, table):
    n_steps = T // PEER_TB
    idx = idx.reshape(n_steps, PEER_TB, P_PAIRS)
    idx = jnp.concatenate([idx, jnp.roll(idx[:, :PEER_G], -1, axis=0)], axis=1)
    blk = pl.BlockSpec((PEER_TB * 8, 128), lambda i: (i, 0))
    return pl.pallas_call(
        _peer_expert_kernel,
        grid=(n_steps,),
        in_specs=[
            pl.BlockSpec((None, PEER_TB + PEER_G, P_PAIRS), lambda i: (i, 0, 0), memory_space=pltpu.SMEM),
            blk,
            pl.BlockSpec((PEER_TB, P_PAIRS), lambda i: (i, 0)),
            blk,
            pl.BlockSpec((None, 8, 128), lambda i: ((i * PEER_TB) // SEG, 0, 0)),
            pl.BlockSpec((P_PAIRS, PEER_LANES), lambda i: (0, 0)),
            pl.BlockSpec(memory_space=pl.ANY),
        ],
        out_specs=blk,
        out_shape=jax.ShapeDtypeStruct((T * 8, 128), F32),
        scratch_shapes=[pltpu.VMEM((PEER_ROWS, 8, 128), jnp.uint32),
                        pltpu.VMEM((PEER_ROWS, 8, 128), jnp.uint32),
                        pltpu.VMEM((PEER_TB, PEER_LANES), F32),
                        pltpu.SemaphoreType.DMA((2, PEER_G))],
        compiler_params=_params("arbitrary"),
        name="peer_expert",
    )(idx, h_tiles, g, x_tiles, gate_tiles, e2, table)


def _pack_expert_table(u, v):
    def tiles(t):
        bits = lax.bitcast_convert_type(t.astype(BF16), jnp.uint16).astype(jnp.uint32)
        bits = bits.reshape(N_EXPERTS, 4, 2, 128)
        return bits[:, :, 0, :] | (bits[:, :, 1, :] << 16)
    return jnp.concatenate([tiles(u), tiles(v)], axis=1)


def kernel(x_prompt, x_sample, cache_k, cache_v, c, c_ctx, ada_w, ada_b, norm1_w, norm2_w,
           a_w_in, a_w_s, a_b_s, a_v_norm, a_w_out, b_w_qkv, b_q_norm, b_k_norm, b_rpb,
           b_w_out, peer_w_q, peer_sub_keys, peer_u, peer_v):
    n_prompt = x_prompt.shape[0]
    n_sample = x_sample.shape[0]
    x = jnp.concatenate([x_prompt.reshape(-1, D), x_sample.reshape(-1, D)], axis=0)

    cond8 = jnp.zeros((8, D), F32).at[0].set(c_ctx).at[1:1 + n_sample].set(c)
    mod = _ada_mod(cond8, ada_w, ada_b)

    head_of = jnp.arange(D) // HEAD_DIM
    ind = (head_of[:, None] == jnp.arange(128)[None, :]).astype(BF16)
    indt = ind.T
    pair_expand = (jnp.arange(P_PAIRS)[:, None] == jnp.arange(PEER_LANES)[None, :] // 16).astype(BF16)

    new_k, new_v = [], []
    for i in range(DEPTH):
        j = i // 2
        mod_l = mod[i]
        n1 = norm1_w[i].reshape(1, D)
        if i % 2 == 0:
            bexp = jnp.repeat(a_b_s[j].T, D // A_GROUPS, axis=1)
            x = _mix_a(x, mod_l, n1, a_w_in[j].astype(BF16), a_w_s[j].astype(BF16), bexp,
                       a_v_norm[j].reshape(1, D), a_w_out[j].astype(BF16))
        else:
            qn = jnp.tile(b_q_norm[j], HEADS).reshape(1, D)
            kn = jnp.tile(b_k_norm[j], HEADS).reshape(1, D)
            q, k, v, kf, vf = _qkv(x, mod_l, n1, b_w_qkv[j].astype(BF16), qn, kn, ind, indt)
            new_k.append(kf[:SEG].reshape(n_prompt, SEQ, HEADS, HEAD_DIM))
            new_v.append(vf[:SEG].reshape(n_prompt, SEQ, HEADS, HEAD_DIM))
            w_out = b_w_out[j].astype(BF16)
            x_ctx = _ctx_attn(q, k, v, x, mod_l, w_out)
            kc = cache_k[:, j].reshape(n_sample, SEQ, D).astype(BF16)
            vc = cache_v[:, j].reshape(n_sample, SEQ, D).astype(BF16)
            x = _na_attn(q, k, v, kc, vc, _na_bias_table(b_rpb[j]), x_ctx, mod_l, w_out)
        h, idx_t, g_t = _peer_select(x, mod_l, norm2_w[i].reshape(1, D), peer_w_q[i].T.astype(BF16),
                                     peer_sub_keys[i].astype(BF16))
        x = _peer_expert(idx_t.T, h.reshape(T * 8, 128), g_t.T, x.reshape(T * 8, 128),
                         mod_l[:, 5].reshape(8, 8, 128), pair_expand,
                         _pack_expert_table(peer_u[i], peer_v[i])).reshape(T, D)

    y_prompt = x[:SEG].reshape(x_prompt.shape)
    y_sample = x[SEG:].reshape(x_sample.shape)
    return (y_prompt, y_sample, jnp.stack(new_k, axis=1), jnp.stack(new_v, axis=1))
```

```python
import functools

import jax
import jax.numpy as jnp
import numpy as np
from jax import lax
from jax.experimental import pallas as pl
from jax.experimental.pallas import tpu as pltpu

F32 = jnp.float32
BF16 = jnp.bfloat16

D = 1024
DEPTH = 4
SEG = 4096
N_SEG = 3
T = N_SEG * SEG
SEQ = 256
GRID_W = 64
CHUNK = 128
A_GROUPS = 8
HEADS = 16
HEAD_DIM = 64
NA_ROWS = 8
NA_COLS = 16
P_HEADS = 8
P_HALF = 128
P_KEYS = 128
P_TOPK = 16
P_PAIRS = P_HEADS * P_TOPK
N_EXPERTS = P_KEYS * P_KEYS
EPS = 1e-6
NEG = -1e30

VMEM_LIMIT = 56 * 1024 * 1024

TM = 256
PEER_TB = 64
PEER_G = 8
PEER_ROWS = PEER_G * P_PAIRS


def _params(*sem):
    return pltpu.CompilerParams(dimension_semantics=sem, vmem_limit_bytes=VMEM_LIMIT)


def _dot(a, b):
    return jnp.dot(a, b, preferred_element_type=F32)


def _dot_nt(a, b):
    return lax.dot_general(a, b, (((1,), (1,)), ((), ())), preferred_element_type=F32)


def _split(a):
    hi = a.astype(BF16)
    lo = (a - hi.astype(F32)).astype(BF16)
    return hi, lo


def _dot_f32(a, b):
    ah, al = _split(a)
    bh, bl = _split(b)
    return _dot(ah, bh) + _dot(ah, bl) + _dot(al, bh)


def _dot_f32_exact_rhs(a, b_bf16):
    ah, al = _split(a)
    return _dot(ah, b_bf16) + _dot(al, b_bf16)


def _rms_mod(x, norm_w, scale, shift):
    y = x * lax.rsqrt(jnp.mean(x * x, axis=-1, keepdims=True) + EPS)
    return (y * norm_w) * (1.0 + scale) + shift


def _rows_from_tiles(ref, n):
    return jnp.concatenate([ref[pl.ds(c, n, stride=8), :] for c in range(D // 128)], axis=1)


def _tile_block(n, index_map):
    return pl.BlockSpec((n * 8, 128), index_map)


def _ada_kernel(cond_ref, w_ref, b_ref, o_ref):
    c = cond_ref[...]
    s = c * jax.nn.sigmoid(c)
    o_ref[...] = _dot_f32(s, w_ref[...]) + b_ref[...]


def _ada_mod(cond8, ada_w, ada_b):
    tn = 1024
    n = 6 * D
    out = pl.pallas_call(
        _ada_kernel,
        grid=(DEPTH, n // tn),
        in_specs=[
            pl.BlockSpec((8, D), lambda l, j: (0, 0)),
            pl.BlockSpec((None, D, tn), lambda l, j: (l, 0, j)),
            pl.BlockSpec((None, 1, tn), lambda l, j: (l, 0, j)),
        ],
        out_specs=pl.BlockSpec((None, 8, tn), lambda l, j: (l, 0, j)),
        out_shape=jax.ShapeDtypeStruct((DEPTH, 8, n), F32),
        compiler_params=_params("parallel", "parallel"),
        name="ada_mod",
    )(cond8, ada_w, ada_b.reshape(DEPTH, 1, n))
    return out.reshape(DEPTH, 8, 6, D)


def _mix_a_kernel(x_ref, mod_ref, n1_ref, win_ref, ws_ref, bexp_ref, vn_ref, wout_ref, o_ref):
    x = _rows_from_tiles(x_ref, TM)
    h = _rms_mod(x, n1_ref[...], mod_ref[1:2, :], mod_ref[0:1, :])
    z = jax.nn.gelu(_dot(h.astype(BF16), win_ref[...]))
    u = z[:, :D]
    v = z[:, D:]
    v = v * lax.rsqrt(jnp.mean(v * v, axis=-1, keepdims=True) + EPS) * vn_ref[...]
    vb = v.astype(BF16)
    rows = []
    for c in range(TM // CHUNK):
        cols = []
        for g in range(A_GROUPS):
            vg = vb[c * CHUNK:(c + 1) * CHUNK, g * 128:(g + 1) * 128]
            cols.append(_dot(ws_ref[g], vg))
        rows.append(jnp.concatenate(cols, axis=1) + bexp_ref[...])
    sv = jnp.concatenate(rows, axis=0)
    mix = _dot((u * sv).astype(BF16), wout_ref[...])
    o_ref[...] = x + mod_ref[2:3, :] * mix


def _mix_a(x, mod_l, n1, w_in, w_s, bexp, vn, w_out):
    const2 = lambda i: (0, 0)
    return pl.pallas_call(
        _mix_a_kernel,
        grid=(T // TM,),
        in_specs=[
            _tile_block(TM, lambda i: (i, 0)),
            pl.BlockSpec((None, 6, D), lambda i: ((i * TM) // SEG, 0, 0)),
            pl.BlockSpec((1, D), const2),
            pl.BlockSpec((D, 2 * D), const2),
            pl.BlockSpec((A_GROUPS, CHUNK, CHUNK), lambda i: (0, 0, 0)),
            pl.BlockSpec((CHUNK, D), const2),
            pl.BlockSpec((1, D), const2),
            pl.BlockSpec((D, D), const2),
        ],
        out_specs=pl.BlockSpec((TM, D), lambda i: (i, 0)),
        out_shape=jax.ShapeDtypeStruct((T, D), F32),
        compiler_params=_params("parallel"),
        name="mix_a",
    )(x, mod_l, n1, w_in, w_s, bexp, vn, w_out)


def _qkv_kernel(x_ref, mod_ref, n1_ref, w_ref, qn_ref, kn_ref, ind_ref, indt_ref,
                q_o, k_o, v_o, kf_o, vf_o):
    x = _rows_from_tiles(x_ref, TM)
    h = _rms_mod(x, n1_ref[...], mod_ref[1:2, :], mod_ref[0:1, :])
    qkv =_dot(h.astype(BF16), w_ref[...])

    def head_norm(t, w):
        ss = _dot_f32_exact_rhs(t * t, ind_ref[...])
        inv = lax.rsqrt(ss * (1.0 / HEAD_DIM) + EPS)
        return t * _dot_f32_exact_rhs(inv, indt_ref[...]) * w

    q = head_norm(qkv[:, :D], qn_ref[...])
    k = head_norm(qkv[:, D:2 * D], kn_ref[...])
    v = qkv[:, 2 * D:]
    q_o[...] = (q * (HEAD_DIM ** -0.5)).astype(BF16)
    k_o[...] = k.astype(BF16)
    v_o[...] = v.astype(BF16)
    kf_o[...] = k
    vf_o[...] = v


def _qkv(x, mod_l, n1, w_qkv, qn, kn, ind, indt):
    const2 = lambda i: (0, 0)
    blk = pl.BlockSpec((TM, D), lambda i: (i, 0))
    return pl.pallas_call(
        _qkv_kernel,
        grid=(T // TM,),
        in_specs=[
            _tile_block(TM, lambda i: (i, 0)),
            pl.BlockSpec((None, 6, D), lambda i: ((i * TM) // SEG, 0, 0)),
            pl.BlockSpec((1, D), const2),
            pl.BlockSpec((D, 3 * D), const2),
            pl.BlockSpec((1, D), const2),
            pl.BlockSpec((1, D), const2),
            pl.BlockSpec((D, 128), const2),
            pl.BlockSpec((128, D), const2),
        ],
        out_specs=[blk, blk, blk, blk, blk],
        out_shape=[jax.ShapeDtypeStruct((T, D), BF16)] * 3 + [jax.ShapeDtypeStruct((T, D), F32)] * 2,
        compiler_params=_params("parallel"),
        name="qkv",
    )(x, mod_l, n1, w_qkv, qn, kn, ind, indt)


def _head_pair_attention(q2, score_parts):
    lane = lax.broadcasted_iota(jnp.int32, (1, 128), 1)
    lo = lane < HEAD_DIM
    out = None
    for sub in range(2):
        keep = lo if sub == 0 else jnp.logical_not(lo)
        qm = jnp.where(keep, q2, jnp.zeros_like(q2))
        scores = []
        for k2, _, bias_fn in score_parts:
            s = _dot_nt(qm, k2)
            b = bias_fn(sub)
            scores.append(s if b is None else s + b)
        m = scores[0].max(axis=-1, keepdims=True)
        for s in scores[1:]:
            m = jnp.maximum(m, s.max(axis=-1, keepdims=True))
        den = None
        acc = None
        for s, (_, v2, _) in zip(scores, score_parts):
            p = jnp.exp(s - m)
            d = p.sum(axis=-1, keepdims=True)
            o = _dot(p.astype(BF16), v2)
            den = d if den is None else den + d
            acc = o if acc is None else acc + o
        o = acc / den
        out = o if sub == 0 else jnp.where(lo, out, o)
    return out


def _ctx_attn_kernel(q_ref, k_ref, v_ref, x_ref, mod_ref, wout_ref, o_ref):
    outs = []
    for hp in range(HEADS // 2):
        sl = slice(hp * 128, (hp + 1) * 128)
        outs.append(_head_pair_attention(
            q_ref[:, sl], [(k_ref[:, sl], v_ref[:, sl], lambda sub: None)]))
    o = jnp.concatenate(outs, axis=1).astype(BF16)
    o_ref[...] = x_ref[...] + mod_ref[2:3, :] * _dot(o, wout_ref[...])


def _ctx_attn(q, k, v, x, mod_l, w_out):
    n = SEG // SEQ
    blk = pl.BlockSpec((SEQ, D), lambda b: (b, 0))
    return pl.pallas_call(
        _ctx_attn_kernel,
        grid=(n,),
        in_specs=[blk, blk, blk, blk,
                  pl.BlockSpec((None, 6, D), lambda b: (0, 0, 0)),
                  pl.BlockSpec((D, D), lambda b: (0, 0))],
        out_specs=blk,
        out_shape=jax.ShapeDtypeStruct((T, D), F32),
        input_output_aliases={3: 0},
        compiler_params=_params("parallel"),
        name="ctx_attn",
    )(q, k, v, x, mod_l, w_out)


def _na_window_start(r):
    rows = SEG // GRID_W
    return jnp.clip(r - NA_ROWS // 2, 0, rows - NA_ROWS)


def _na_attn_kernel(q_ref, k_ref, v_ref, kc_ref, vc_ref, bias_ref, x_ref, mod_ref, wout_ref, o_ref):
    r = pl.program_id(1)
    start = pl.multiple_of(_na_window_start(r) * GRID_W, GRID_W)
    n_loc = NA_ROWS * GRID_W
    outs = []
    for hp in range(HEADS // 2):
        sl = slice(hp * 128, (hp + 1) * 128)
        kw = k_ref[pl.ds(start, n_loc), sl]
        vw = v_ref[pl.ds(start, n_loc), sl]
        outs.append(_head_pair_attention(
            q_ref[:, sl],
            [(kw, vw, lambda sub, hp=hp: bias_ref[2 * hp + sub]),
             (kc_ref[:, sl], vc_ref[:, sl], lambda sub: None)]))
    o = jnp.concatenate(outs, axis=1).astype(BF16)
    o_ref[...] = x_ref[...] + mod_ref[2:3, :] * _dot(o, wout_ref[...])


def _na_attn(q, k, v, kc, vc, bias, x, mod_l, w_out):
    rows = SEG // GRID_W
    nb = (T - SEG) // SEG
    qblk = pl.BlockSpec((GRID_W, D), lambda b, r: (rows + b * rows + r, 0))
    img = pl.BlockSpec((SEG, D), lambda b, r: (1 + b, 0))
    ctx = pl.BlockSpec((None, SEQ, D), lambda b, r: (b, 0, 0))

    def bias_map(b, r):
        return (_na_window_start(r) - r + NA_ROWS - 1, 0, 0, 0)

    return pl.pallas_call(
        _na_attn_kernel,
        grid=(nb, rows),
        in_specs=[qblk, img, img, ctx, ctx,
                  pl.BlockSpec((None, HEADS, GRID_W, NA_ROWS * GRID_W), bias_map),
                  qblk,
                  pl.BlockSpec((None, 6, D), lambda b, r: (1 + b, 0, 0)),
                  pl.BlockSpec((D, D), lambda b, r: (0, 0))],
        out_specs=qblk,
        out_shape=jax.ShapeDtypeStruct((T, D), F32),
        input_output_aliases={6: 0},
        compiler_params=_params("parallel", "arbitrary"),
        name="na_attn",
    )(q, k, v, kc, vc, bias, x, mod_l, w_out)


def _na_bias_table(rpb):
    qc = np.arange(GRID_W)
    c0 = np.clip(qc - NA_COLS // 2, 0, GRID_W - NA_COLS)
    kc = np.arange(GRID_W)
    inside = (kc[None, :] >= c0[:, None]) & (kc[None, :] < c0[:, None] + NA_COLS)
    dc = kc[None, :] - qc[:, None] + NA_COLS - 1
    cols = jnp.zeros(rpb.shape[:2] + (GRID_W, GRID_W), F32)
    for d in range(2 * NA_COLS - 1):
        cols = jnp.where(jnp.asarray(inside & (dc == d))[None, None], rpb[:, :, d, None, None], cols)
    cols = jnp.where(jnp.asarray(inside)[None, None], cols, NEG)
    tab = jnp.stack([cols[:, var:var + NA_ROWS] for var in range(NA_ROWS)], axis=0)
    tab = jnp.transpose(tab, (0, 1, 3, 2, 4))
    return tab.reshape(NA_ROWS, HEADS, GRID_W, NA_ROWS * GRID_W)


def _peer_select_kernel(x_ref, mod_ref, n2_ref, wqt_ref, sk_ref, h_o, xt_o, idx_o, g_o,
                        q_scr, idx_scr, g_scr):
    x = x_ref[...]
    h = _rms_mod(x, n2_ref[...], mod_ref[4:5, :], mod_ref[3:4, :])
    for c in range(D // 128):
        h_o[pl.ds(c, TM, stride=8), :] = h[:, c * 128:(c + 1) * 128]
        xt_o[pl.ds(c, TM, stride=8), :] = x[:, c * 128:(c + 1) * 128]
    q_scr[...] = _dot_nt(wqt_ref[...], h.astype(BF16))

    n_c = P_TOPK * P_TOPK
    key_pos = lax.broadcasted_iota(jnp.int32, (P_KEYS, TM), 0).astype(F32)
    rank = lax.broadcasted_iota(jnp.int32, (P_TOPK, TM), 0)
    rank8 = lax.broadcasted_iota(jnp.int32, (8, TM), 0)
    cand_pos = jnp.concatenate(
        [rank] + [a * P_TOPK + rank8 for a in range(1, 8)] + [(8 + rank8) * P_TOPK], axis=0).astype(F32)
    neg_inf = jnp.float32(-jnp.inf)

    def head(hd, carry):
        top_val, top_idx = [], []
        for p in range(2):
            off = pl.multiple_of(hd * 2 * P_HALF + p * P_HALF, P_HALF)
            s = _dot(sk_ref[p], q_scr[pl.ds(off, P_HALF), :].astype(BF16))
            vals, idxs = [], []
            for it in range(P_TOPK):
                m = jnp.max(s, axis=0, keepdims=True)
                am = jnp.min(jnp.where(s == m, key_pos, float(P_KEYS)), axis=0, keepdims=True)
                s = jnp.where(key_pos == am, neg_inf, s)
                vals.append(m)
                idxs.append(am)
            top_val.append(vals)
            top_idx.append(idxs)
        s2 = jnp.zeros((P_TOPK, TM), F32)
        i2 = jnp.zeros((P_TOPK, TM), F32)
        for it in range(P_TOPK):
            s2 = jnp.where(rank == it, top_val[1][it], s2)
            i2 = jnp.where(rank == it, top_idx[1][it], i2)
        s1_hi = jnp.zeros((8, TM), F32)
        i1_hi = jnp.zeros((8, TM), F32)
        for it in range(8, P_TOPK):
            s1_hi = jnp.where(rank8 == it - 8, top_val[0][it], s1_hi)
            i1_hi = jnp.where(rank8 == it - 8, top_idx[0][it], i1_hi)
        cs_parts = [top_val[0][0] + s2]
        ce_parts = [top_idx[0][0] * float(P_KEYS) + i2]
        for a in range(1, 8):
            live = rank8 < P_TOPK // (a + 1)
            cs_parts.append(jnp.where(live, top_val[0][a] + s2[:8], neg_inf))
            ce_parts.append(top_idx[0][a] * float(P_KEYS) + i2[:8])
        cs_parts.append(s1_hi + top_val[1][0])
        ce_parts.append(i1_hi * float(P_KEYS) + top_idx[1][0])
        cs = jnp.concatenate(cs_parts, axis=0)
        ce = jnp.concatenate(ce_parts, axis=0)
        sc = jnp.zeros((P_TOPK, TM), F32)
        ex_id = jnp.zeros((P_TOPK, TM), F32)
        best = None
        for it in range(P_TOPK):
            m = jnp.max(cs, axis=0, keepdims=True)
            am = jnp.min(jnp.where(cs == m, cand_pos, float(n_c)), axis=0, keepdims=True)
            hit = cand_pos == am
            e = jnp.max(jnp.where(hit, ce, -1.0), axis=0, keepdims=True)
            cs = jnp.where(hit, neg_inf, cs)
            sc = jnp.where(rank == it, m, sc)
            ex_id = jnp.where(rank == it, e, ex_id)
            if it == 0:
                best = m
        ex = jnp.exp(sc - best)
        rows = pl.ds(pl.multiple_of(hd * P_TOPK, P_TOPK), P_TOPK)
        idx_scr[rows, :] = ex_id
        g_scr[rows, :] = ex / jnp.sum(ex, axis=0, keepdims=True)
        return carry

    lax.fori_loop(0, P_HEADS, head, 0)
    idx_o[...] = idx_scr[...].T.astype(jnp.int32)
    g_o[...] = g_scr[...].T


def _peer_select(x, mod_l, n2, wqt, sk):
    const2 = lambda i: (0, 0)
    blk = pl.BlockSpec((TM, D), lambda i: (i, 0))
    tiles = _tile_block(TM, lambda i: (i, 0))
    tblk = pl.BlockSpec((TM, P_PAIRS), lambda i: (i, 0))
    return pl.pallas_call(
        _peer_select_kernel,
        grid=(T // TM,),
        in_specs=[
            blk,
            pl.BlockSpec((None, 6, D), lambda i: ((i * TM) // SEG, 0, 0)),
            pl.BlockSpec((1, D), const2),
            pl.BlockSpec((P_HEADS * 2 * P_HALF, D), const2),
            pl.BlockSpec((2, P_KEYS, P_HALF), lambda i: (0, 0, 0)),
        ],
        out_specs=[tiles, tiles, tblk, tblk],
        out_shape=[jax.ShapeDtypeStruct((T * 8, 128), F32),
                   jax.ShapeDtypeStruct((T * 8, 128), F32),
                   jax.ShapeDtypeStruct((T, P_PAIRS), jnp.int32),
                   jax.ShapeDtypeStruct((T, P_PAIRS), F32)],
        scratch_shapes=[pltpu.VMEM((P_HEADS * 2 * P_HALF, TM), F32),
                        pltpu.VMEM((P_PAIRS, TM), F32),
                        pltpu.VMEM((P_PAIRS, TM), F32)],
        compiler_params=_params("parallel"),
        name="peer_select",
    )(x, mod_l, n2, wqt, sk)


PEER_LANES = 2 * P_HEADS * P_PAIRS


def _peer_expert_kernel(idx_ref, h_ref, g_ref, x_ref, gate_ref, e2_ref, tab_ref, o_ref,
                        buf0, buf1, grep, sem):
    step = pl.program_id(0)
    n_batches = PEER_TB // PEER_G
    bufs = (buf0, buf1)

    def gather_wait(slot, j):
        rows = pl.ds(j * P_PAIRS, P_PAIRS)
        pltpu.make_async_copy(tab_ref.at[rows], bufs[slot].at[rows], sem.at[slot, j]).wait()

    def gather_start(tok, j, slot):
        for k in range(P_PAIRS):
            pltpu.make_async_copy(
                tab_ref.at[idx_ref[tok, k]],
                bufs[slot].at[j * P_PAIRS + k],
                sem.at[slot, j]).start(priority=k % 2)

    @pl.when(step == 0)
    def _():
        for j in range(PEER_G):
            gather_start(j, j, 0)

    grep[...] = _dot_f32_exact_rhs(g_ref[...], e2_ref[...])

    sub = lax.broadcasted_iota(jnp.int32, (8, PEER_LANES), 0)
    lane = lax.broadcasted_iota(jnp.int32, (8, PEER_LANES), 1)
    is_u = (lane % 16) == sub
    is_v = (lane % 16) == sub + 8

    def token_rows(slot, j):
        tiles = bufs[slot][j * P_PAIRS:(j + 1) * P_PAIRS]
        return pltpu.bitcast(tiles.reshape(P_PAIRS * 8, 128), BF16)

    def tile_rows(b, j):
        return pl.ds(pl.multiple_of((b * PEER_G + j) * 8, 8), 8)

    def combine(b, slot, j, w):
        lj = jnp.where(is_v, jnp.broadcast_to(w[j:j + 1, :], (8, PEER_LANES)), 0.0)
        y = _dot(lj.astype(BF16), token_rows(slot, j))
        o_ref[tile_rows(b, j), :] = x_ref[tile_rows(b, j), :] + gate_ref[...] * y

    def batch(b, slot):
        part = jnp.zeros((PEER_G, PEER_LANES), F32)
        for j in range(PEER_G):
            gather_start((b + 1) * PEER_G + j, j, 1 - slot)
            gather_wait(slot, j)
            r = _dot_nt(h_ref[tile_rows(b, j), :].astype(BF16), token_rows(slot, j))
            c = jnp.sum(jnp.where(is_u, r, 0.0), axis=0, keepdims=True)
            part = jnp.where(sub == j, c, part)
        a = part
        for k in (1, 2, 4, 8):
            up = pltpu.roll(a, k, 1)
            down = pltpu.roll(a, PEER_LANES - k, 1)
            a = a + jnp.where((lane & k) != 0, up, down)
        w = jax.nn.gelu(a) * grep[pl.ds(pl.multiple_of(b * PEER_G, PEER_G), PEER_G), :]
        for j in range(PEER_G):
            combine(b, slot, j, w)

    def batch_pair(i, carry):
        batch(2 * i, 0)
        batch(2 * i + 1, 1)
        return carry

    lax.fori_loop(0, n_batches // 2, batch_pair, 0)

    @pl.when(step == pl.num_programs(0) - 1)
    def _():
        for j in range(PEER_G):
            gather_wait(0, j)


def _peer_expert(idx, h_tiles, g, x_tiles, gate_tiles, e2, table):
    n_steps = T // PEER_TB
    idx = idx.reshape(n_steps, PEER_TB, P_PAIRS)
    idx = jnp.concatenate([idx, jnp.roll(idx[:, :PEER_G], -1, axis=0)], axis=1)
    blk = pl.BlockSpec((PEER_TB * 8, 128), lambda i: (i, 0))
    return pl.pallas_call(
        _peer_expert_kernel,
        grid=(n_steps,),
        in_specs=[
            pl.BlockSpec((None, PEER_TB + PEER_G, P_PAIRS), lambda i: (i, 0, 0), memory_space=pltpu.SMEM),
            blk,
            pl.BlockSpec((PEER_TB, P_PAIRS), lambda i: (i, 0)),
            blk,
            pl.BlockSpec((None, 8, 128), lambda i: ((i * PEER_TB) // SEG, 0, 0)),
            pl.BlockSpec((P_PAIRS, PEER_LANES), lambda i: (0, 0)),
            pl.BlockSpec(memory_space=pl.ANY),
        ],
        out_specs=blk,
        out_shape=jax.ShapeDtypeStruct((T * 8, 128), F32),
        scratch_shapes=[pltpu.VMEM((PEER_ROWS, 8, 128), jnp.uint32),
                        pltpu.VMEM((PEER_ROWS, 8, 128), jnp.uint32),
                        pltpu.VMEM((PEER_TB, PEER_LANES), F32),
                        pltpu.SemaphoreType.DMA((2, PEER_G))],
        compiler_params=_params("arbitrary"),
        name="peer_expert",
    )(idx, h_tiles, g, x_tiles, gate_tiles, e2, table)


def _pack_expert_table(u, v):
    def tiles(t):
        pairs = jnp.swapaxes(t.astype(BF16).reshape(N_EXPERTS, 4, 2, 128), 2, 3)
        return lax.bitcast_convert_type(pairs, jnp.uint32)
    return jnp.concatenate([tiles(u), tiles(v)], axis=1)


def kernel(x_prompt, x_sample, cache_k, cache_v, c, c_ctx, ada_w, ada_b, norm1_w, norm2_w,
           a_w_in, a_w_s, a_b_s, a_v_norm, a_w_out, b_w_qkv, b_q_norm, b_k_norm, b_rpb,
           b_w_out, peer_w_q, peer_sub_keys, peer_u, peer_v):
    n_prompt = x_prompt.shape[0]
    n_sample = x_sample.shape[0]
    xt = jnp.concatenate([x_prompt.reshape(-1, D), x_sample.reshape(-1, D)], axis=0).reshape(T * 8, 128)

    cond8 = jnp.zeros((8, D), F32).at[0].set(c_ctx).at[1:1 + n_sample].set(c)
    mod = _ada_mod(cond8, ada_w, ada_b)

    head_of = jnp.arange(D) // HEAD_DIM
    ind = (head_of[:, None] == jnp.arange(128)[None, :]).astype(BF16)
    indt = ind.T
    pair_expand = (jnp.arange(P_PAIRS)[:, None] == jnp.arange(PEER_LANES)[None, :] // 16).astype(BF16)

    new_k, new_v = [], []
    for i in range(DEPTH):
        j = i // 2
        mod_l = mod[i]
        n1 = norm1_w[i].reshape(1, D)
        if i % 2 == 0:
            bexp = jnp.repeat(a_b_s[j].T, D // A_GROUPS, axis=1)
            x = _mix_a(xt, mod_l, n1, a_w_in[j].astype(BF16), a_w_s[j].astype(BF16), bexp,
                       a_v_norm[j].reshape(1, D), a_w_out[j].astype(BF16))
        else:
            qn = jnp.tile(b_q_norm[j], HEADS).reshape(1, D)
            kn = jnp.tile(b_k_norm[j], HEADS).reshape(1, D)
            q, k, v, kf, vf = _qkv(xt, mod_l, n1, b_w_qkv[j].astype(BF16), qn, kn, ind, indt)
            new_k.append(kf[:SEG].reshape(n_prompt, SEQ, HEADS, HEAD_DIM))
            new_v.append(vf[:SEG].reshape(n_prompt, SEQ, HEADS, HEAD_DIM))
            w_out = b_w_out[j].astype(BF16)
            x_ctx = _ctx_attn(q, k, v, xt.reshape(T, D), mod_l, w_out)
            kc = cache_k[:, j].reshape(n_sample, SEQ, D).astype(BF16)
            vc = cache_v[:, j].reshape(n_sample, SEQ, D).astype(BF16)
            x = _na_attn(q, k, v, kc, vc, _na_bias_table(b_rpb[j]), x_ctx, mod_l, w_out)
        ht, xt, idx, g = _peer_select(x, mod_l, norm2_w[i].reshape(1, D), peer_w_q[i].T.astype(BF16),
                                      peer_sub_keys[i].astype(BF16))
        xt = _peer_expert(idx, ht, g, xt, mod_l[:, 5].reshape(8, 8, 128), pair_expand,
                          _pack_expert_table(peer_u[i], peer_v[i]))

    x = xt.reshape(T, D)
    y_prompt = x[:SEG].reshape(x_prompt.shape)
    y_sample = x[SEG:].reshape(x_sample.shape)
    return (y_prompt, y_sample, jnp.stack(new_k, axis=1), jnp.stack(new_v, axis=1))
```

```python
import functools

import jax
import jax.numpy as jnp
import numpy as np
from jax import lax
from jax.experimental import pallas as pl
from jax.experimental.pallas import tpu as pltpu

F32 = jnp.float32
BF16 = jnp.bfloat16

D = 1024
DEPTH = 4
SEG = 4096
N_SEG = 3
T = N_SEG * SEG
SEQ = 256
GRID_W = 64
CHUNK = 128
A_GROUPS = 8
HEADS = 16
HEAD_DIM = 64
NA_ROWS = 8
NA_COLS = 16
P_HEADS = 8
P_HALF = 128
P_KEYS = 128
P_TOPK = 16
P_PAIRS = P_HEADS * P_TOPK
N_EXPERTS = P_KEYS * P_KEYS
EPS = 1e-6
NEG = -1e30

VMEM_LIMIT = 56 * 1024 * 1024

TM = 256
PEER_TB = 64
PEER_G = 8
PEER_ROWS = PEER_G * P_PAIRS


def _params(*sem):
    return pltpu.CompilerParams(dimension_semantics=sem, vmem_limit_bytes=VMEM_LIMIT)


def _dot(a, b):
    return jnp.dot(a, b, preferred_element_type=F32)


def _dot_nt(a, b):
    return lax.dot_general(a, b, (((1,), (1,)), ((), ())), preferred_element_type=F32)


def _split(a):
    hi = a.astype(BF16)
    lo = (a - hi.astype(F32)).astype(BF16)
    return hi, lo


def _dot_f32(a, b):
    ah, al = _split(a)
    bh, bl = _split(b)
    return _dot(ah, bh) + _dot(ah, bl) + _dot(al, bh)


def _dot_f32_exact_rhs(a, b_bf16):
    ah, al = _split(a)
    return _dot(ah, b_bf16) + _dot(al, b_bf16)


def _rms_mod(x, norm_w, scale, shift):
    y = x * lax.rsqrt(jnp.mean(x * x, axis=-1, keepdims=True) + EPS)
    return (y * norm_w) * (1.0 + scale) + shift


def _rows_from_tiles(ref, n):
    return jnp.concatenate([ref[pl.ds(c, n, stride=8), :] for c in range(D // 128)], axis=1)


def _tile_block(n, index_map):
    return pl.BlockSpec((n * 8, 128), index_map)


def _ada_kernel(cond_ref, w_ref, b_ref, o_ref):
    c = cond_ref[...]
    s = c * jax.nn.sigmoid(c)
    o_ref[...] = _dot_f32(s, w_ref[...]) + b_ref[...]


def _ada_mod(cond8, ada_w, ada_b):
    tn = 1024
    n = 6 * D
    out = pl.pallas_call(
        _ada_kernel,
        grid=(DEPTH, n // tn),
        in_specs=[
            pl.BlockSpec((8, D), lambda l, j: (0, 0)),
            pl.BlockSpec((None, D, tn), lambda l, j: (l, 0, j)),
            pl.BlockSpec((None, 1, tn), lambda l, j: (l, 0, j)),
        ],
        out_specs=pl.BlockSpec((None, 8, tn), lambda l, j: (l, 0, j)),
        out_shape=jax.ShapeDtypeStruct((DEPTH, 8, n), F32),
        compiler_params=_params("parallel", "parallel"),
        name="ada_mod",
    )(cond8, ada_w, ada_b.reshape(DEPTH, 1, n))
    return out.reshape(DEPTH, 8, 6, D)


def _mix_a_kernel(x_ref, mod_ref, n1_ref, win_ref, ws_ref, bexp_ref, vn_ref, wout_ref, o_ref):
    x = _rows_from_tiles(x_ref, TM)
    h = _rms_mod(x, n1_ref[...], mod_ref[1:2, :], mod_ref[0:1, :])
    z = jax.nn.gelu(_dot(h.astype(BF16), win_ref[...]))
    u = z[:, :D]
    v = z[:, D:]
    v = v * lax.rsqrt(jnp.mean(v * v, axis=-1, keepdims=True) + EPS) * vn_ref[...]
    vb = v.astype(BF16)
    rows = []
    for c in range(TM // CHUNK):
        cols = []
        for g in range(A_GROUPS):
            vg = vb[c * CHUNK:(c + 1) * CHUNK, g * 128:(g + 1) * 128]
            cols.append(_dot(ws_ref[g], vg))
        rows.append(jnp.concatenate(cols, axis=1) + bexp_ref[...])
    sv = jnp.concatenate(rows, axis=0)
    mix = _dot((u * sv).astype(BF16), wout_ref[...])
    o_ref[...] = x + mod_ref[2:3, :] * mix


def _mix_a(x, mod_l, n1, w_in, w_s, bexp, vn, w_out):
    const2 = lambda i: (0, 0)
    return pl.pallas_call(
        _mix_a_kernel,
        grid=(T // TM,),
        in_specs=[
            _tile_block(TM, lambda i: (i, 0)),
            pl.BlockSpec((None, 6, D), lambda i: ((i * TM) // SEG, 0, 0)),
            pl.BlockSpec((1, D), const2),
            pl.BlockSpec((D, 2 * D), const2),
            pl.BlockSpec((A_GROUPS, CHUNK, CHUNK), lambda i: (0, 0, 0)),
            pl.BlockSpec((CHUNK, D), const2),
            pl.BlockSpec((1, D), const2),
            pl.BlockSpec((D, D), const2),
        ],
        out_specs=pl.BlockSpec((TM, D), lambda i: (i, 0)),
        out_shape=jax.ShapeDtypeStruct((T, D), F32),
        compiler_params=_params("parallel"),
        name="mix_a",
    )(x, mod_l, n1, w_in, w_s, bexp, vn, w_out)


def _qkv_kernel(x_ref, mod_ref, n1_ref, w_ref, qn_ref, kn_ref, ind_ref, indt_ref,
                q_o, k_o, v_o, kf_o, vf_o):
    x = _rows_from_tiles(x_ref, TM)
    h = _rms_mod(x, n1_ref[...], mod_ref[1:2, :], mod_ref[0:1, :])
    qkv =_dot(h.astype(BF16), w_ref[...])

    def head_norm(t, w):
        ss = _dot_f32_exact_rhs(t * t, ind_ref[...])
        inv = lax.rsqrt(ss * (1.0 / HEAD_DIM) + EPS)
        return t * _dot_f32_exact_rhs(inv, indt_ref[...]) * w

    q = head_norm(qkv[:, :D], qn_ref[...])
    k = head_norm(qkv[:, D:2 * D], kn_ref[...])
    v = qkv[:, 2 * D:]
    q_o[...] = (q * (HEAD_DIM ** -0.5)).astype(BF16)
    k_o[...] = k.astype(BF16)
    v_o[...] = v.astype(BF16)
    kf_o[...] = k
    vf_o[...] = v


def _qkv(x, mod_l, n1, w_qkv, qn, kn, ind, indt):
    const2 = lambda i: (0, 0)
    blk = pl.BlockSpec((TM, D), lambda i: (i, 0))
    return pl.pallas_call(
        _qkv_kernel,
        grid=(T // TM,),
        in_specs=[
            _tile_block(TM, lambda i: (i, 0)),
            pl.BlockSpec((None, 6, D), lambda i: ((i * TM) // SEG, 0, 0)),
            pl.BlockSpec((1, D), const2),
            pl.BlockSpec((D, 3 * D), const2),
            pl.BlockSpec((1, D), const2),
            pl.BlockSpec((1, D), const2),
            pl.BlockSpec((D, 128), const2),
            pl.BlockSpec((128, D), const2),
        ],
        out_specs=[blk, blk, blk, blk, blk],
        out_shape=[jax.ShapeDtypeStruct((T, D), BF16)] * 3 + [jax.ShapeDtypeStruct((T, D), F32)] * 2,
        compiler_params=_params("parallel"),
        name="qkv",
    )(x, mod_l, n1, w_qkv, qn, kn, ind, indt)


def _head_pair_attention(q2, score_parts):
    lane = lax.broadcasted_iota(jnp.int32, (1, 128), 1)
    lo = lane < HEAD_DIM
    out = None
    for sub in range(2):
        keep = lo if sub == 0 else jnp.logical_not(lo)
        qm = jnp.where(keep, q2, jnp.zeros_like(q2))
        scores = []
        for k2, _, bias_fn in score_parts:
            s = _dot_nt(qm, k2)
            b = bias_fn(sub)
            scores.append(s if b is None else s + b)
        m = scores[0].max(axis=-1, keepdims=True)
        for s in scores[1:]:
            m = jnp.maximum(m, s.max(axis=-1, keepdims=True))
        den = None
        acc = None
        for s, (_, v2, _) in zip(scores, score_parts):
            p = jnp.exp(s - m)
            d = p.sum(axis=-1, keepdims=True)
            o = _dot(p.astype(BF16), v2)
            den = d if den is None else den + d
            acc = o if acc is None else acc + o
        o = acc / den
        out = o if sub == 0 else jnp.where(lo, out, o)
    return out


def _ctx_attn_kernel(q_ref, k_ref, v_ref, x_ref, mod_ref, wout_ref, o_ref):
    outs = []
    for hp in range(HEADS // 2):
        sl = slice(hp * 128, (hp + 1) * 128)
        outs.append(_head_pair_attention(
            q_ref[:, sl], [(k_ref[:, sl], v_ref[:, sl], lambda sub: None)]))
    o = jnp.concatenate(outs, axis=1).astype(BF16)
    o_ref[...] = x_ref[...] + mod_ref[2:3, :] * _dot(o, wout_ref[...])


def _ctx_attn(q, k, v, x, mod_l, w_out):
    n = SEG // SEQ
    blk = pl.BlockSpec((SEQ, D), lambda b: (b, 0))
    return pl.pallas_call(
        _ctx_attn_kernel,
        grid=(n,),
        in_specs=[blk, blk, blk, blk,
                  pl.BlockSpec((None, 6, D), lambda b: (0, 0, 0)),
                  pl.BlockSpec((D, D), lambda b: (0, 0))],
        out_specs=blk,
        out_shape=jax.ShapeDtypeStruct((T, D), F32),
        input_output_aliases={3: 0},
        compiler_params=_params("parallel"),
        name="ctx_attn",
    )(q, k, v, x, mod_l, w_out)


def _na_window_start(r):
    rows = SEG // GRID_W
    return jnp.clip(r - NA_ROWS // 2, 0, rows - NA_ROWS)


def _na_attn_kernel(q_ref, k_ref, v_ref, kc_ref, vc_ref, bias_ref, x_ref, mod_ref, wout_ref, o_ref):
    r = pl.program_id(1)
    start = pl.multiple_of(_na_window_start(r) * GRID_W, GRID_W)
    n_loc = NA_ROWS * GRID_W
    outs = []
    for hp in range(HEADS // 2):
        sl = slice(hp * 128, (hp + 1) * 128)
        kw = k_ref[pl.ds(start, n_loc), sl]
        vw = v_ref[pl.ds(start, n_loc), sl]
        outs.append(_head_pair_attention(
            q_ref[:, sl],
            [(kw, vw, lambda sub, hp=hp: bias_ref[2 * hp + sub]),
             (kc_ref[:, sl], vc_ref[:, sl], lambda sub: None)]))
    o = jnp.concatenate(outs, axis=1).astype(BF16)
    o_ref[...] = x_ref[...] + mod_ref[2:3, :] * _dot(o, wout_ref[...])


def _na_attn(q, k, v, kc, vc, bias, x, mod_l, w_out):
    rows = SEG // GRID_W
    nb = (T - SEG) // SEG
    qblk = pl.BlockSpec((GRID_W, D), lambda b, r: (rows + b * rows + r, 0))
    img = pl.BlockSpec((SEG, D), lambda b, r: (1 + b, 0))
    ctx = pl.BlockSpec((None, SEQ, D), lambda b, r: (b, 0, 0))

    def bias_map(b, r):
        return (_na_window_start(r) - r + NA_ROWS - 1, 0, 0, 0)

    return pl.pallas_call(
        _na_attn_kernel,
        grid=(nb, rows),
        in_specs=[qblk, img, img, ctx, ctx,
                  pl.BlockSpec((None, HEADS, GRID_W, NA_ROWS * GRID_W), bias_map),
                  qblk,
                  pl.BlockSpec((None, 6, D), lambda b, r: (1 + b, 0, 0)),
                  pl.BlockSpec((D, D), lambda b, r: (0, 0))],
        out_specs=qblk,
        out_shape=jax.ShapeDtypeStruct((T, D), F32),
        input_output_aliases={6: 0},
        compiler_params=_params("parallel", "arbitrary"),
        name="na_attn",
    )(q, k, v, kc, vc, bias, x, mod_l, w_out)


def _na_bias_table(rpb):
    qc = np.arange(GRID_W)
    c0 = np.clip(qc - NA_COLS // 2, 0, GRID_W - NA_COLS)
    kc = np.arange(GRID_W)
    inside = (kc[None, :] >= c0[:, None]) & (kc[None, :] < c0[:, None] + NA_COLS)
    dc = kc[None, :] - qc[:, None] + NA_COLS - 1
    cols = jnp.zeros(rpb.shape[:2] + (GRID_W, GRID_W), F32)
    for d in range(2 * NA_COLS - 1):
        cols = jnp.where(jnp.asarray(inside & (dc == d))[None, None], rpb[:, :, d, None, None], cols)
    cols = jnp.where(jnp.asarray(inside)[None, None], cols, NEG)
    tab = jnp.stack([cols[:, var:var + NA_ROWS] for var in range(NA_ROWS)], axis=0)
    tab = jnp.transpose(tab, (0, 1, 3, 2, 4))
    return tab.reshape(NA_ROWS, HEADS, GRID_W, NA_ROWS * GRID_W)


def _peer_select_kernel(x_ref, mod_ref, n2_ref, wqt_ref, sk_ref, h_o, xt_o, idx_o, g_o,
                        q_scr, idx_scr, g_scr):
    x = x_ref[...]
    h = _rms_mod(x, n2_ref[...], mod_ref[4:5, :], mod_ref[3:4, :])
    for c in range(D // 128):
        h_o[pl.ds(c, TM, stride=8), :] = h[:, c * 128:(c + 1) * 128]
        xt_o[pl.ds(c, TM, stride=8), :] = x[:, c * 128:(c + 1) * 128]
    q_scr[...] = _dot_nt(wqt_ref[...], h.astype(BF16))

    n_c = P_TOPK * P_TOPK
    key_pos = lax.broadcasted_iota(jnp.int32, (P_KEYS, TM), 0).astype(F32)
    rank = lax.broadcasted_iota(jnp.int32, (P_TOPK, TM), 0)
    rank8 = lax.broadcasted_iota(jnp.int32, (8, TM), 0)
    cand_pos = jnp.concatenate(
        [rank] + [a * P_TOPK + rank8 for a in range(1, 8)] + [(8 + rank8) * P_TOPK], axis=0).astype(F32)
    neg_inf = jnp.float32(-jnp.inf)

    def head(hd, carry):
        top_val, top_idx = [], []
        for p in range(2):
            off = pl.multiple_of(hd * 2 * P_HALF + p * P_HALF, P_HALF)
            s = _dot(sk_ref[p], q_scr[pl.ds(off, P_HALF), :].astype(BF16))
            vals, idxs = [], []
            for it in range(P_TOPK):
                m = jnp.max(s, axis=0, keepdims=True)
                am = jnp.min(jnp.where(s == m, key_pos, float(P_KEYS)), axis=0, keepdims=True)
                s = jnp.where(key_pos == am, neg_inf, s)
                vals.append(m)
                idxs.append(am)
            top_val.append(vals)
            top_idx.append(idxs)
        s2 = jnp.zeros((P_TOPK, TM), F32)
        i2 = jnp.zeros((P_TOPK, TM), F32)
        for it in range(P_TOPK):
            s2 = jnp.where(rank == it, top_val[1][it], s2)
            i2 = jnp.where(rank == it, top_idx[1][it], i2)
        s1_hi = jnp.zeros((8, TM), F32)
        i1_hi = jnp.zeros((8, TM), F32)
        for it in range(8, P_TOPK):
            s1_hi = jnp.where(rank8 == it - 8, top_val[0][it], s1_hi)
            i1_hi = jnp.where(rank8 == it - 8, top_idx[0][it], i1_hi)
        cs_parts = [top_val[0][0] + s2]
        ce_parts = [top_idx[0][0] * float(P_KEYS) + i2]
        for a in range(1, 8):
            live = rank8 < P_TOPK // (a + 1)
            cs_parts.append(jnp.where(live, top_val[0][a] + s2[:8], neg_inf))
            ce_parts.append(top_idx[0][a] * float(P_KEYS) + i2[:8])
        cs_parts.append(s1_hi + top_val[1][0])
        ce_parts.append(i1_hi * float(P_KEYS) + top_idx[1][0])
        cs = jnp.concatenate(cs_parts, axis=0)
        ce = jnp.concatenate(ce_parts, axis=0)
        sc = jnp.zeros((P_TOPK, TM), F32)
        ex_id = jnp.zeros((P_TOPK, TM), F32)
        best = None
        for it in range(P_TOPK):
            m = jnp.max(cs, axis=0, keepdims=True)
            am = jnp.min(jnp.where(cs == m, cand_pos, float(n_c)), axis=0, keepdims=True)
            hit = cand_pos == am
            e = jnp.max(jnp.where(hit, ce, -1.0), axis=0, keepdims=True)
            cs = jnp.where(hit, neg_inf, cs)
            sc = jnp.where(rank == it, m, sc)
            ex_id = jnp.where(rank == it, e, ex_id)
            if it == 0:
                best = m
        ex = jnp.exp(sc - best)
        rows = pl.ds(pl.multiple_of(hd * P_TOPK, P_TOPK), P_TOPK)
        idx_scr[rows, :] = ex_id
        g_scr[rows, :] = ex / jnp.sum(ex, axis=0, keepdims=True)
        return carry

    lax.fori_loop(0, P_HEADS, head, 0)
    idx_o[...] = idx_scr[...].T.astype(jnp.int32)
    g_o[...] = g_scr[...].T


def _peer_select(x, mod_l, n2, wqt, sk):
    const2 = lambda i: (0, 0)
    blk = pl.BlockSpec((TM, D), lambda i: (i, 0))
    tiles = _tile_block(TM, lambda i: (i, 0))
    tblk = pl.BlockSpec((TM, P_PAIRS), lambda i: (i, 0))
    return pl.pallas_call(
        _peer_select_kernel,
        grid=(T // TM,),
        in_specs=[
            blk,
            pl.BlockSpec((None, 6, D), lambda i: ((i * TM) // SEG, 0, 0)),
            pl.BlockSpec((1, D), const2),
            pl.BlockSpec((P_HEADS * 2 * P_HALF, D), const2),
            pl.BlockSpec((2, P_KEYS, P_HALF), lambda i: (0, 0, 0)),
        ],
        out_specs=[tiles, tiles, tblk, tblk],
        out_shape=[jax.ShapeDtypeStruct((T * 8, 128), F32),
                   jax.ShapeDtypeStruct((T * 8, 128), F32),
                   jax.ShapeDtypeStruct((T, P_PAIRS), jnp.int32),
                   jax.ShapeDtypeStruct((T, P_PAIRS), F32)],
        scratch_shapes=[pltpu.VMEM((P_HEADS * 2 * P_HALF, TM), F32),
                        pltpu.VMEM((P_PAIRS, TM), F32),
                        pltpu.VMEM((P_PAIRS, TM), F32)],
        compiler_params=_params("parallel"),
        name="peer_select",
    )(x, mod_l, n2, wqt, sk)


PEER_LANES = 2 * P_HEADS * P_PAIRS


def _peer_expert_kernel(idx_ref, h_ref, g_ref, x_ref, gate_ref, e2_ref, tab_ref, o_ref,
                        buf0, buf1, grep, sem):
    step = pl.program_id(0)
    n_batches = PEER_TB // PEER_G
    bufs = (buf0, buf1)

    def gather_wait(slot, j):
        rows = pl.ds(j * P_PAIRS, P_PAIRS)
        pltpu.make_async_copy(tab_ref.at[rows], bufs[slot].at[rows], sem.at[slot, j]).wait()

    def gather_start(tok, j, slot):
        for k in range(P_PAIRS):
            pltpu.make_async_copy(
                tab_ref.at[idx_ref[tok, k]],
                bufs[slot].at[j * P_PAIRS + k],
                sem.at[slot, j]).start(priority=k % 2)

    @pl.when(step == 0)
    def _():
        for j in range(PEER_G):
            gather_start(j, j, 0)

    grep[...] = _dot_f32_exact_rhs(g_ref[...], e2_ref[...])

    sub = lax.broadcasted_iota(jnp.int32, (8, PEER_LANES), 0)
    lane = lax.broadcasted_iota(jnp.int32, (8, PEER_LANES), 1)
    is_u = (lane % 16) == sub
    is_v = (lane % 16) == sub + 8

    def token_rows(slot, j):
        tiles = bufs[slot][j * P_PAIRS:(j + 1) * P_PAIRS]
        return pltpu.bitcast(tiles.reshape(P_PAIRS * 8, 128), BF16)

    def tile_rows(b, j):
        return pl.ds(pl.multiple_of((b * PEER_G + j) * 8, 8), 8)

    def combine(b, slot, j, w):
        lj = jnp.where(is_v, jnp.broadcast_to(w[j:j + 1, :], (8, PEER_LANES)), 0.0)
        y = _dot(lj.astype(BF16), token_rows(slot, j))
        o_ref[tile_rows(b, j), :] = x_ref[tile_rows(b, j), :] + gate_ref[...] * y

    def batch(b, slot):
        part = jnp.zeros((PEER_G, PEER_LANES), F32)
        for j in range(PEER_G):
            gather_start((b + 1) * PEER_G + j, j, 1 - slot)
            gather_wait(slot, j)
            r = _dot_nt(h_ref[tile_rows(b, j), :].astype(BF16), token_rows(slot, j))
            c = jnp.sum(jnp.where(is_u, r, 0.0), axis=0, keepdims=True)
            part = jnp.where(sub == j, c, part)
        a = part
        for k in (1, 2, 4, 8):
            up = pltpu.roll(a, k, 1)
            down = pltpu.roll(a, PEER_LANES - k, 1)
            a = a + jnp.where((lane & k) != 0, up, down)
        w = jax.nn.gelu(a) * grep[pl.ds(pl.multiple_of(b * PEER_G, PEER_G), PEER_G), :]
        for j in range(PEER_G):
            combine(b, slot, j, w)

    def batch_pair(i, carry):
        batch(2 * i, 0)
        batch(2 * i + 1, 1)
        return carry

    lax.fori_loop(0, n_batches // 2, batch_pair, 0)

    @pl.when(step == pl.num_programs(0) - 1)
    def _():
        for j in range(PEER_G):
            gather_wait(0, j)


def _peer_expert(idx, h_tiles, g, x_tiles, gate_tiles, e2, table):
    n_steps = T // PEER_TB
    idx = idx.reshape(n_steps, PEER_TB, P_PAIRS)
    idx = jnp.concatenate([idx, jnp.roll(idx[:, :PEER_G], -1, axis=0)], axis=1)
    blk = pl.BlockSpec((PEER_TB * 8, 128), lambda i: (i, 0))
    return pl.pallas_call(
        _peer_expert_kernel,
        grid=(n_steps,),
        in_specs=[
            pl.BlockSpec((None, PEER_TB + PEER_G, P_PAIRS), lambda i: (i, 0, 0), memory_space=pltpu.SMEM),
            blk,
            pl.BlockSpec((PEER_TB, P_PAIRS), lambda i: (i, 0)),
            blk,
            pl.BlockSpec((None, 8, 128), lambda i: ((i * PEER_TB) // SEG, 0, 0)),
            pl.BlockSpec((P_PAIRS, PEER_LANES), lambda i: (0, 0)),
            pl.BlockSpec(memory_space=pl.ANY),
        ],
        out_specs=blk,
        out_shape=jax.ShapeDtypeStruct((T * 8, 128), F32),
        scratch_shapes=[pltpu.VMEM((PEER_ROWS, 8, 128), jnp.uint32),
                        pltpu.VMEM((PEER_ROWS, 8, 128), jnp.uint32),
                        pltpu.VMEM((PEER_TB, PEER_LANES), F32),
                        pltpu.SemaphoreType.DMA((2, PEER_G))],
        compiler_params=_params("arbitrary"),
        name="peer_expert",
    )(idx, h_tiles, g, x_tiles, gate_tiles, e2, table)


PACK_ROWS = 512


def _pack_kernel(u_ref, v_ref, o_ref):
    def bf16_bits(t):
        return pltpu.bitcast(t.astype(BF16).astype(F32), jnp.uint32)

    for half, ref in enumerate((u_ref, v_ref)):
        for s in range(4):
            lo = bf16_bits(ref[:, (2 * s) * 128:(2 * s + 1) * 128])
            hi = bf16_bits(ref[:, (2 * s + 1) * 128:(2 * s + 2) * 128])
            o_ref[pl.ds(4 * half + s, PACK_ROWS, stride=8), :] = (lo >> 16) | hi


def _pack_expert_table(u_all, v_all, layer):
    blk = pl.BlockSpec((None, PACK_ROWS, D), lambda i: (layer, i, 0))
    out = pl.pallas_call(
        _pack_kernel,
        grid=(N_EXPERTS // PACK_ROWS,),
        in_specs=[blk, blk],
        out_specs=pl.BlockSpec((PACK_ROWS * 8, 128), lambda i: (i, 0)),
        out_shape=jax.ShapeDtypeStruct((N_EXPERTS * 8, 128), jnp.uint32),
        compiler_params=_params("parallel"),
        name="pack_table",
    )(u_all, v_all)
    return out.reshape(N_EXPERTS, 8, 128)


def kernel(x_prompt, x_sample, cache_k, cache_v, c, c_ctx, ada_w, ada_b, norm1_w, norm2_w,
           a_w_in, a_w_s, a_b_s, a_v_norm, a_w_out, b_w_qkv, b_q_norm, b_k_norm, b_rpb,
           b_w_out, peer_w_q, peer_sub_keys, peer_u, peer_v):
    n_prompt = x_prompt.shape[0]
    n_sample = x_sample.shape[0]
    xt = jnp.concatenate([x_prompt.reshape(-1, D), x_sample.reshape(-1, D)], axis=0).reshape(T * 8, 128)

    cond8 = jnp.zeros((8, D), F32).at[0].set(c_ctx).at[1:1 + n_sample].set(c)
    mod = _ada_mod(cond8, ada_w, ada_b)

    head_of = jnp.arange(D) // HEAD_DIM
    ind = (head_of[:, None] == jnp.arange(128)[None, :]).astype(BF16)
    indt = ind.T
    pair_expand = (jnp.arange(P_PAIRS)[:, None] == jnp.arange(PEER_LANES)[None, :] // 16).astype(BF16)

    new_k, new_v = [], []
    for i in range(DEPTH):
        j = i // 2
        mod_l = mod[i]
        n1 = norm1_w[i].reshape(1, D)
        if i % 2 == 0:
            bexp = jnp.repeat(a_b_s[j].T, D // A_GROUPS, axis=1)
            x = _mix_a(xt, mod_l, n1, a_w_in[j].astype(BF16), a_w_s[j].astype(BF16), bexp,
                       a_v_norm[j].reshape(1, D), a_w_out[j].astype(BF16))
        else:
            qn = jnp.tile(b_q_norm[j], HEADS).reshape(1, D)
            kn = jnp.tile(b_k_norm[j], HEADS).reshape(1, D)
            q, k, v, kf, vf = _qkv(xt, mod_l, n1, b_w_qkv[j].astype(BF16), qn, kn, ind, indt)
            new_k.append(kf[:SEG].reshape(n_prompt, SEQ, HEADS, HEAD_DIM))
            new_v.append(vf[:SEG].reshape(n_prompt, SEQ, HEADS, HEAD_DIM))
            w_out = b_w_out[j].astype(BF16)
            x_ctx = _ctx_attn(q, k, v, xt.reshape(T, D), mod_l, w_out)
            kc = cache_k[:, j].reshape(n_sample, SEQ, D).astype(BF16)
            vc = cache_v[:, j].reshape(n_sample, SEQ, D).astype(BF16)
            x = _na_attn(q, k, v, kc, vc, _na_bias_table(b_rpb[j]), x_ctx, mod_l, w_out)
        ht, xt, idx, g = _peer_select(x, mod_l, norm2_w[i].reshape(1, D), peer_w_q[i].T.astype(BF16),
                                      peer_sub_keys[i].astype(BF16))
        xt = _peer_expert(idx, ht, g, xt, mod_l[:, 5].reshape(8, 8, 128), pair_expand,
                          _pack_expert_table(peer_u, peer_v, i))

    x = xt.reshape(T, D)
    y_prompt = x[:SEG].reshape(x_prompt.shape)
    y_sample = x[SEG:].reshape(x_sample.shape)
    return (y_prompt, y_sample, jnp.stack(new_k, axis=1), jnp.stack(new_v, axis=1))
```

```python
import functools

import jax
import jax.numpy as jnp
import numpy as np
from jax import lax
from jax.experimental import pallas as pl
from jax.experimental.pallas import tpu as pltpu

F32 = jnp.float32
BF16 = jnp.bfloat16

D = 1024
DEPTH = 4
SEG = 4096
N_SEG = 3
T = N_SEG * SEG
SEQ = 256
GRID_W = 64
CHUNK = 128
A_GROUPS = 8
HEADS = 16
HEAD_DIM = 64
NA_ROWS = 8
NA_COLS = 16
P_HEADS = 8
P_HALF = 128
P_KEYS = 128
P_TOPK = 16
P_PAIRS = P_HEADS * P_TOPK
N_EXPERTS = P_KEYS * P_KEYS
EPS = 1e-6
NEG = -1e30

VMEM_LIMIT = 56 * 1024 * 1024

TM = 256
PEER_TB = 64
PEER_G = 8
PEER_ROWS = PEER_G * P_PAIRS


def _params(*sem):
    return pltpu.CompilerParams(dimension_semantics=sem, vmem_limit_bytes=VMEM_LIMIT)


def _dot(a, b):
    return jnp.dot(a, b, preferred_element_type=F32)


def _dot_nt(a, b):
    return lax.dot_general(a, b, (((1,), (1,)), ((), ())), preferred_element_type=F32)


def _split(a):
    hi = a.astype(BF16)
    lo = (a - hi.astype(F32)).astype(BF16)
    return hi, lo


def _dot_f32(a, b):
    ah, al = _split(a)
    bh, bl = _split(b)
    return _dot(ah, bh) + _dot(ah, bl) + _dot(al, bh)


def _dot_f32_exact_rhs(a, b_bf16):
    ah, al = _split(a)
    return _dot(ah, b_bf16) + _dot(al, b_bf16)


def _rms_mod(x, norm_w, scale, shift):
    y = x * lax.rsqrt(jnp.mean(x * x, axis=-1, keepdims=True) + EPS)
    return (y * norm_w) * (1.0 + scale) + shift


def _rows_from_tiles(ref, n):
    return jnp.concatenate([ref[pl.ds(c, n, stride=8), :] for c in range(D // 128)], axis=1)


def _tile_block(n, index_map):
    return pl.BlockSpec((n * 8, 128), index_map)


def _ada_kernel(cond_ref, w_ref, b_ref, o_ref):
    c = cond_ref[...]
    s = c * jax.nn.sigmoid(c)
    o_ref[...] = _dot_f32(s, w_ref[...]) + b_ref[...]


def _ada_mod(cond8, ada_w, ada_b):
    tn = 1024
    n = 6 * D
    out = pl.pallas_call(
        _ada_kernel,
        grid=(DEPTH, n // tn),
        in_specs=[
            pl.BlockSpec((8, D), lambda l, j: (0, 0)),
            pl.BlockSpec((None, D, tn), lambda l, j: (l, 0, j)),
            pl.BlockSpec((None, 1, tn), lambda l, j: (l, 0, j)),
        ],
        out_specs=pl.BlockSpec((None, 8, tn), lambda l, j: (l, 0, j)),
        out_shape=jax.ShapeDtypeStruct((DEPTH, 8, n), F32),
        compiler_params=_params("parallel", "parallel"),
        name="ada_mod",
    )(cond8, ada_w, ada_b.reshape(DEPTH, 1, n))
    return out.reshape(DEPTH, 8, 6, D)


def _mix_a_kernel(x_ref, mod_ref, n1_ref, win_ref, ws_ref, bexp_ref, vn_ref, wout_ref, o_ref):
    x = _rows_from_tiles(x_ref, TM)
    h = _rms_mod(x, n1_ref[...], mod_ref[1:2, :], mod_ref[0:1, :])
    z = jax.nn.gelu(_dot(h.astype(BF16), win_ref[...]))
    u = z[:, :D]
    v = z[:, D:]
    v = v * lax.rsqrt(jnp.mean(v * v, axis=-1, keepdims=True) + EPS) * vn_ref[...]
    vb = v.astype(BF16)
    rows = []
    for c in range(TM // CHUNK):
        cols = []
        for g in range(A_GROUPS):
            vg = vb[c * CHUNK:(c + 1) * CHUNK, g * 128:(g + 1) * 128]
            cols.append(_dot(ws_ref[g], vg))
        rows.append(jnp.concatenate(cols, axis=1) + bexp_ref[...])
    sv = jnp.concatenate(rows, axis=0)
    mix = _dot((u * sv).astype(BF16), wout_ref[...])
    o_ref[...] = x + mod_ref[2:3, :] * mix


def _mix_a(x, mod_l, n1, w_in, w_s, bexp, vn, w_out):
    const2 = lambda i: (0, 0)
    return pl.pallas_call(
        _mix_a_kernel,
        grid=(T // TM,),
        in_specs=[
            _tile_block(TM, lambda i: (i, 0)),
            pl.BlockSpec((None, 6, D), lambda i: ((i * TM) // SEG, 0, 0)),
            pl.BlockSpec((1, D), const2),
            pl.BlockSpec((D, 2 * D), const2),
            pl.BlockSpec((A_GROUPS, CHUNK, CHUNK), lambda i: (0, 0, 0)),
            pl.BlockSpec((CHUNK, D), const2),
            pl.BlockSpec((1, D), const2),
            pl.BlockSpec((D, D), const2),
        ],
        out_specs=pl.BlockSpec((TM, D), lambda i: (i, 0)),
        out_shape=jax.ShapeDtypeStruct((T, D), F32),
        compiler_params=_params("parallel"),
        name="mix_a",
    )(x, mod_l, n1, w_in, w_s, bexp, vn, w_out)


def _qkv_kernel(x_ref, mod_ref, n1_ref, w_ref, qn_ref, kn_ref, ind_ref, indt_ref,
                q_o, k_o, v_o, kf_o, vf_o):
    x = _rows_from_tiles(x_ref, TM)
    h = _rms_mod(x, n1_ref[...], mod_ref[1:2, :], mod_ref[0:1, :])
    qkv =_dot(h.astype(BF16), w_ref[...])

    def head_norm(t, w):
        ss = _dot_f32_exact_rhs(t * t, ind_ref[...])
        inv = lax.rsqrt(ss * (1.0 / HEAD_DIM) + EPS)
        return t * _dot_f32_exact_rhs(inv, indt_ref[...]) * w

    q = head_norm(qkv[:, :D], qn_ref[...])
    k = head_norm(qkv[:, D:2 * D], kn_ref[...])
    v = qkv[:, 2 * D:]
    q_o[...] = (q * (HEAD_DIM ** -0.5)).astype(BF16)
    k_o[...] = k.astype(BF16)
    v_o[...] = v.astype(BF16)
    kf_o[...] = k
    vf_o[...] = v


def _qkv(x, mod_l, n1, w_qkv, qn, kn, ind, indt):
    const2 = lambda i: (0, 0)
    blk = pl.BlockSpec((TM, D), lambda i: (i, 0))
    return pl.pallas_call(
        _qkv_kernel,
        grid=(T // TM,),
        in_specs=[
            _tile_block(TM, lambda i: (i, 0)),
            pl.BlockSpec((None, 6, D), lambda i: ((i * TM) // SEG, 0, 0)),
            pl.BlockSpec((1, D), const2),
            pl.BlockSpec((D, 3 * D), const2),
            pl.BlockSpec((1, D), const2),
            pl.BlockSpec((1, D), const2),
            pl.BlockSpec((D, 128), const2),
            pl.BlockSpec((128, D), const2),
        ],
        out_specs=[blk, blk, blk, blk, blk],
        out_shape=[jax.ShapeDtypeStruct((T, D), BF16)] * 3 + [jax.ShapeDtypeStruct((T, D), F32)] * 2,
        compiler_params=_params("parallel"),
        name="qkv",
    )(x, mod_l, n1, w_qkv, qn, kn, ind, indt)


def _head_pair_attention(q2, score_parts):
    lane = lax.broadcasted_iota(jnp.int32, (1, 128), 1)
    lo = lane < HEAD_DIM
    out = None
    for sub in range(2):
        keep = lo if sub == 0 else jnp.logical_not(lo)
        qm = jnp.where(keep, q2, jnp.zeros_like(q2))
        scores = []
        for k2, _, bias_fn in score_parts:
            s = _dot_nt(qm, k2)
            b = bias_fn(sub)
            scores.append(s if b is None else s + b)
        m = scores[0].max(axis=-1, keepdims=True)
        for s in scores[1:]:
            m = jnp.maximum(m, s.max(axis=-1, keepdims=True))
        den = None
        acc = None
        for s, (_, v2, _) in zip(scores, score_parts):
            p = jnp.exp(s - m)
            d = p.sum(axis=-1, keepdims=True)
            o = _dot(p.astype(BF16), v2)
            den = d if den is None else den + d
            acc = o if acc is None else acc + o
        o = acc / den
        out = o if sub == 0 else jnp.where(lo, out, o)
    return out


def _ctx_attn_kernel(q_ref, k_ref, v_ref, x_ref, mod_ref, wout_ref, o_ref):
    outs = []
    for hp in range(HEADS // 2):
        sl = slice(hp * 128, (hp + 1) * 128)
        outs.append(_head_pair_attention(
            q_ref[:, sl], [(k_ref[:, sl], v_ref[:, sl], lambda sub: None)]))
    o = jnp.concatenate(outs, axis=1).astype(BF16)
    o_ref[...] = x_ref[...] + mod_ref[2:3, :] * _dot(o, wout_ref[...])


def _ctx_attn(q, k, v, x, mod_l, w_out):
    n = SEG // SEQ
    blk = pl.BlockSpec((SEQ, D), lambda b: (b, 0))
    return pl.pallas_call(
        _ctx_attn_kernel,
        grid=(n,),
        in_specs=[blk, blk, blk, blk,
                  pl.BlockSpec((None, 6, D), lambda b: (0, 0, 0)),
                  pl.BlockSpec((D, D), lambda b: (0, 0))],
        out_specs=blk,
        out_shape=jax.ShapeDtypeStruct((T, D), F32),
        input_output_aliases={3: 0},
        compiler_params=_params("parallel"),
        name="ctx_attn",
    )(q, k, v, x, mod_l, w_out)


def _na_window_start(r):
    rows = SEG // GRID_W
    return jnp.clip(r - NA_ROWS // 2, 0, rows - NA_ROWS)


def _na_attn_kernel(q_ref, k_ref, v_ref, kc_ref, vc_ref, bias_ref, x_ref, mod_ref, wout_ref, o_ref):
    r = pl.program_id(1)
    start = pl.multiple_of(_na_window_start(r) * GRID_W, GRID_W)
    n_loc = NA_ROWS * GRID_W
    outs = []
    for hp in range(HEADS // 2):
        sl = slice(hp * 128, (hp + 1) * 128)
        kw = k_ref[pl.ds(start, n_loc), sl]
        vw = v_ref[pl.ds(start, n_loc), sl]
        outs.append(_head_pair_attention(
            q_ref[:, sl],
            [(kw, vw, lambda sub, hp=hp: bias_ref[2 * hp + sub]),
             (kc_ref[:, sl], vc_ref[:, sl], lambda sub: None)]))
    o = jnp.concatenate(outs, axis=1).astype(BF16)
    o_ref[...] = x_ref[...] + mod_ref[2:3, :] * _dot(o, wout_ref[...])


def _na_attn(q, k, v, kc, vc, bias, x, mod_l, w_out):
    rows = SEG // GRID_W
    nb = (T - SEG) // SEG
    qblk = pl.BlockSpec((GRID_W, D), lambda b, r: (rows + b * rows + r, 0))
    img = pl.BlockSpec((SEG, D), lambda b, r: (1 + b, 0))
    ctx = pl.BlockSpec((None, SEQ, D), lambda b, r: (b, 0, 0))

    def bias_map(b, r):
        return (_na_window_start(r) - r + NA_ROWS - 1, 0, 0, 0)

    return pl.pallas_call(
        _na_attn_kernel,
        grid=(nb, rows),
        in_specs=[qblk, img, img, ctx, ctx,
                  pl.BlockSpec((None, HEADS, GRID_W, NA_ROWS * GRID_W), bias_map),
                  qblk,
                  pl.BlockSpec((None, 6, D), lambda b, r: (1 + b, 0, 0)),
                  pl.BlockSpec((D, D), lambda b, r: (0, 0))],
        out_specs=qblk,
        out_shape=jax.ShapeDtypeStruct((T, D), F32),
        input_output_aliases={6: 0},
        compiler_params=_params("parallel", "arbitrary"),
        name="na_attn",
    )(q, k, v, kc, vc, bias, x, mod_l, w_out)


def _na_bias_table(rpb):
    qc = np.arange(GRID_W)
    c0 = np.clip(qc - NA_COLS // 2, 0, GRID_W - NA_COLS)
    kc = np.arange(GRID_W)
    inside = (kc[None, :] >= c0[:, None]) & (kc[None, :] < c0[:, None] + NA_COLS)
    dc = kc[None, :] - qc[:, None] + NA_COLS - 1
    cols = jnp.zeros(rpb.shape[:2] + (GRID_W, GRID_W), F32)
    for d in range(2 * NA_COLS - 1):
        cols = jnp.where(jnp.asarray(inside & (dc == d))[None, None], rpb[:, :, d, None, None], cols)
    cols = jnp.where(jnp.asarray(inside)[None, None], cols, NEG)
    tab = jnp.stack([cols[:, var:var + NA_ROWS] for var in range(NA_ROWS)], axis=0)
    tab = jnp.transpose(tab, (0, 1, 3, 2, 4))
    return tab.reshape(NA_ROWS, HEADS, GRID_W, NA_ROWS * GRID_W)


def _peer_select_kernel(x_ref, mod_ref, n2_ref, wqt_ref, sk_ref, h_o, xt_o, idx_o, g_o,
                        q_scr, idx_scr, g_scr):
    x = x_ref[...]
    h = _rms_mod(x, n2_ref[...], mod_ref[4:5, :], mod_ref[3:4, :])
    for c in range(D // 128):
        h_o[pl.ds(c, TM, stride=8), :] = h[:, c * 128:(c + 1) * 128]
        xt_o[pl.ds(c, TM, stride=8), :] = x[:, c * 128:(c + 1) * 128]
    q_scr[...] = _dot_nt(wqt_ref[...], h.astype(BF16))

    n_c = P_TOPK * P_TOPK
    key_pos = lax.broadcasted_iota(jnp.int32, (P_KEYS, TM), 0).astype(F32)
    rank = lax.broadcasted_iota(jnp.int32, (P_TOPK, TM), 0)
    rank8 = lax.broadcasted_iota(jnp.int32, (8, TM), 0)
    cand_pos = jnp.concatenate(
        [rank] + [a * P_TOPK + rank8 for a in range(1, 8)] + [(8 + rank8) * P_TOPK], axis=0).astype(F32)
    neg_inf = jnp.float32(-jnp.inf)

    def head(hd, carry):
        top_val, top_idx = [], []
        for p in range(2):
            off = pl.multiple_of(hd * 2 * P_HALF + p * P_HALF, P_HALF)
            s = _dot(sk_ref[p], q_scr[pl.ds(off, P_HALF), :].astype(BF16))
            vals, idxs = [], []
            for it in range(P_TOPK):
                m = jnp.max(s, axis=0, keepdims=True)
                am = jnp.min(jnp.where(s == m, key_pos, float(P_KEYS)), axis=0, keepdims=True)
                s = jnp.where(key_pos == am, neg_inf, s)
                vals.append(m)
                idxs.append(am)
            top_val.append(vals)
            top_idx.append(idxs)
        s2 = jnp.zeros((P_TOPK, TM), F32)
        i2 = jnp.zeros((P_TOPK, TM), F32)
        for it in range(P_TOPK):
            s2 = jnp.where(rank == it, top_val[1][it], s2)
            i2 = jnp.where(rank == it, top_idx[1][it], i2)
        s1_hi = jnp.zeros((8, TM), F32)
        i1_hi = jnp.zeros((8, TM), F32)
        for it in range(8, P_TOPK):
            s1_hi = jnp.where(rank8 == it - 8, top_val[0][it], s1_hi)
            i1_hi = jnp.where(rank8 == it - 8, top_idx[0][it], i1_hi)
        cs_parts = [top_val[0][0] + s2]
        ce_parts = [top_idx[0][0] * float(P_KEYS) + i2]
        for a in range(1, 8):
            live = rank8 < P_TOPK // (a + 1)
            cs_parts.append(jnp.where(live, top_val[0][a] + s2[:8], neg_inf))
            ce_parts.append(top_idx[0][a] * float(P_KEYS) + i2[:8])
        cs_parts.append(s1_hi + top_val[1][0])
        ce_parts.append(i1_hi * float(P_KEYS) + top_idx[1][0])
        cs = jnp.concatenate(cs_parts, axis=0)
        ce = jnp.concatenate(ce_parts, axis=0)
        sc = jnp.zeros((P_TOPK, TM), F32)
        ex_id = jnp.zeros((P_TOPK, TM), F32)
        best = None
        for it in range(P_TOPK):
            m = jnp.max(cs, axis=0, keepdims=True)
            am = jnp.min(jnp.where(cs == m, cand_pos, float(n_c)), axis=0, keepdims=True)
            hit = cand_pos == am
            e = jnp.max(jnp.where(hit, ce, -1.0), axis=0, keepdims=True)
            cs = jnp.where(hit, neg_inf, cs)
            sc = jnp.where(rank == it, m, sc)
            ex_id = jnp.where(rank == it, e, ex_id)
            if it == 0:
                best = m
        ex = jnp.exp(sc - best)
        rows = pl.ds(pl.multiple_of(hd * P_TOPK, P_TOPK), P_TOPK)
        idx_scr[rows, :] = ex_id
        g_scr[rows, :] = ex / jnp.sum(ex, axis=0, keepdims=True)
        return carry

    lax.fori_loop(0, P_HEADS, head, 0)
    idx_o[...] = idx_scr[...].T.astype(jnp.int32)
    g_o[...] = g_scr[...].T


def _peer_select(x, mod_l, n2, wqt, sk):
    const2 = lambda i: (0, 0)
    blk = pl.BlockSpec((TM, D), lambda i: (i, 0))
    tiles = _tile_block(TM, lambda i: (i, 0))
    tblk = pl.BlockSpec((TM, P_PAIRS), lambda i: (i, 0))
    return pl.pallas_call(
        _peer_select_kernel,
        grid=(T // TM,),
        in_specs=[
            blk,
            pl.BlockSpec((None, 6, D), lambda i: ((i * TM) // SEG, 0, 0)),
            pl.BlockSpec((1, D), const2),
            pl.BlockSpec((P_HEADS * 2 * P_HALF, D), const2),
            pl.BlockSpec((2, P_KEYS, P_HALF), lambda i: (0, 0, 0)),
        ],
        out_specs=[tiles, tiles, tblk, tblk],
        out_shape=[jax.ShapeDtypeStruct((T * 8, 128), F32),
                   jax.ShapeDtypeStruct((T * 8, 128), F32),
                   jax.ShapeDtypeStruct((T, P_PAIRS), jnp.int32),
                   jax.ShapeDtypeStruct((T, P_PAIRS), F32)],
        scratch_shapes=[pltpu.VMEM((P_HEADS * 2 * P_HALF, TM), F32),
                        pltpu.VMEM((P_PAIRS, TM), F32),
                        pltpu.VMEM((P_PAIRS, TM), F32)],
        compiler_params=_params("parallel"),
        name="peer_select",
    )(x, mod_l, n2, wqt, sk)


PEER_LANES = 2 * P_HEADS * P_PAIRS
PEER_EARLY = 96


def _peer_expert_kernel(idx_ref, h_ref, g_ref, x_ref, gate_ref, e2_ref, tab_ref, o_ref,
                        buf0, buf1, grep, sem):
    step = pl.program_id(0)
    n_batches = PEER_TB // PEER_G
    bufs = (buf0, buf1)

    def gather_wait(slot, j):
        rows = pl.ds(j * P_PAIRS, P_PAIRS)
        pltpu.make_async_copy(tab_ref.at[rows], bufs[slot].at[rows], sem.at[slot, j]).wait()

    def gather_start(tok, j, slot, k0=0, k1=P_PAIRS):
        for k in range(k0, k1):
            pltpu.make_async_copy(
                tab_ref.at[idx_ref[tok, k]],
                bufs[slot].at[j * P_PAIRS + k],
                sem.at[slot, j]).start(priority=k % 2)

    @pl.when(step == 0)
    def _():
        for j in range(PEER_G):
            gather_start(j, j, 0)

    grep[...] = _dot_f32_exact_rhs(g_ref[...], e2_ref[...])

    sub = lax.broadcasted_iota(jnp.int32, (8, PEER_LANES), 0)
    lane = lax.broadcasted_iota(jnp.int32, (8, PEER_LANES), 1)
    is_u = (lane % 16) == sub
    is_v = (lane % 16) == sub + 8

    def token_rows(slot, j):
        tiles = bufs[slot][j * P_PAIRS:(j + 1) * P_PAIRS]
        return pltpu.bitcast(tiles.reshape(P_PAIRS * 8, 128), BF16)

    def tile_rows(b, j):
        return pl.ds(pl.multiple_of((b * PEER_G + j) * 8, 8), 8)

    def combine(b, slot, j, w):
        lj = jnp.where(is_v, jnp.broadcast_to(w[j:j + 1, :], (8, PEER_LANES)), 0.0)
        y = _dot(lj.astype(BF16), token_rows(slot, j))
        o_ref[tile_rows(b, j), :] = x_ref[tile_rows(b, j), :] + gate_ref[...] * y

    def batch(b, slot):
        part = jnp.zeros((PEER_G, PEER_LANES), F32)
        nxt = (b + 1) * PEER_G
        for j0 in range(0, PEER_G, 2):
            for j in (j0, j0 + 1):
                gather_start(nxt + j, j, 1 - slot, 0, PEER_EARLY)
            for j in (j0, j0 + 1):
                gather_wait(slot, j)
            for j in (j0, j0 + 1):
                r = _dot_nt(h_ref[tile_rows(b, j), :].astype(BF16), token_rows(slot, j))
                c = jnp.sum(jnp.where(is_u, r, 0.0), axis=0, keepdims=True)
                part = jnp.where(sub == j, c, part)
        a = part
        for k in (1, 2, 4, 8):
            up = pltpu.roll(a, k, 1)
            down = pltpu.roll(a, PEER_LANES - k, 1)
            a = a + jnp.where((lane & k) != 0, up, down)
        w = jax.nn.gelu(a) * grep[pl.ds(pl.multiple_of(b * PEER_G, PEER_G), PEER_G), :]
        for j in range(PEER_G):
            gather_start(nxt + j, j, 1 - slot, PEER_EARLY, P_PAIRS)
            combine(b, slot, j, w)

    def batch_pair(i, carry):
        batch(2 * i, 0)
        batch(2 * i + 1, 1)
        return carry

    lax.fori_loop(0, n_batches // 2, batch_pair, 0)

    @pl.when(step == pl.num_programs(0) - 1)
    def _():
        for j in range(PEER_G):
            gather_wait(0, j)


def _peer_expert(idx, h_tiles, g, x_tiles, gate_tiles, e2, table):
    n_steps = T // PEER_TB
    idx = idx.reshape(n_steps, PEER_TB, P_PAIRS)
    idx = jnp.concatenate([idx, jnp.roll(idx[:, :PEER_G], -1, axis=0)], axis=1)
    blk = pl.BlockSpec((PEER_TB * 8, 128), lambda i: (i, 0))
    return pl.pallas_call(
        _peer_expert_kernel,
        grid=(n_steps,),
        in_specs=[
            pl.BlockSpec((None, PEER_TB + PEER_G, P_PAIRS), lambda i: (i, 0, 0), memory_space=pltpu.SMEM),
            blk,
            pl.BlockSpec((PEER_TB, P_PAIRS), lambda i: (i, 0)),
            blk,
            pl.BlockSpec((None, 8, 128), lambda i: ((i * PEER_TB) // SEG, 0, 0)),
            pl.BlockSpec((P_PAIRS, PEER_LANES), lambda i: (0, 0)),
            pl.BlockSpec(memory_space=pl.ANY),
        ],
        out_specs=blk,
        out_shape=jax.ShapeDtypeStruct((T * 8, 128), F32),
        scratch_shapes=[pltpu.VMEM((PEER_ROWS, 8, 128), jnp.uint32),
                        pltpu.VMEM((PEER_ROWS, 8, 128), jnp.uint32),
                        pltpu.VMEM((PEER_TB, PEER_LANES), F32),
                        pltpu.SemaphoreType.DMA((2, PEER_G))],
        compiler_params=_params("arbitrary"),
        name="peer_expert",
    )(idx, h_tiles, g, x_tiles, gate_tiles, e2, table)


PACK_ROWS = 512


def _pack_kernel(u_ref, v_ref, o_ref):
    def bf16_bits(t):
        return pltpu.bitcast(t.astype(BF16).astype(F32), jnp.uint32)

    for half, ref in enumerate((u_ref, v_ref)):
        for s in range(4):
            lo = bf16_bits(ref[:, (2 * s) * 128:(2 * s + 1) * 128])
            hi = bf16_bits(ref[:, (2 * s + 1) * 128:(2 * s + 2) * 128])
            o_ref[pl.ds(4 * half + s, PACK_ROWS, stride=8), :] = (lo >> 16) | hi


def _pack_expert_table(u_all, v_all, layer):
    blk = pl.BlockSpec((None, PACK_ROWS, D), lambda i: (layer, i, 0))
    out = pl.pallas_call(
        _pack_kernel,
        grid=(N_EXPERTS // PACK_ROWS,),
        in_specs=[blk, blk],
        out_specs=pl.BlockSpec((PACK_ROWS * 8, 128), lambda i: (i, 0)),
        out_shape=jax.ShapeDtypeStruct((N_EXPERTS * 8, 128), jnp.uint32),
        compiler_params=_params("parallel"),
        name="pack_table",
    )(u_all, v_all)
    return out.reshape(N_EXPERTS, 8, 128)


def kernel(x_prompt, x_sample, cache_k, cache_v, c, c_ctx, ada_w, ada_b, norm1_w, norm2_w,
           a_w_in, a_w_s, a_b_s, a_v_norm, a_w_out, b_w_qkv, b_q_norm, b_k_norm, b_rpb,
           b_w_out, peer_w_q, peer_sub_keys, peer_u, peer_v):
    n_prompt = x_prompt.shape[0]
    n_sample = x_sample.shape[0]
    xt = jnp.concatenate([x_prompt.reshape(-1, D), x_sample.reshape(-1, D)], axis=0).reshape(T * 8, 128)

    cond8 = jnp.zeros((8, D), F32).at[0].set(c_ctx).at[1:1 + n_sample].set(c)
    mod = _ada_mod(cond8, ada_w, ada_b)

    head_of = jnp.arange(D) // HEAD_DIM
    ind = (head_of[:, None] == jnp.arange(128)[None, :]).astype(BF16)
    indt = ind.T
    pair_expand = (jnp.arange(P_PAIRS)[:, None] == jnp.arange(PEER_LANES)[None, :] // 16).astype(BF16)

    new_k, new_v = [], []
    for i in range(DEPTH):
        j = i // 2
        mod_l = mod[i]
        n1 = norm1_w[i].reshape(1, D)
        if i % 2 == 0:
            bexp = jnp.repeat(a_b_s[j].T, D // A_GROUPS, axis=1)
            x = _mix_a(xt, mod_l, n1, a_w_in[j].astype(BF16), a_w_s[j].astype(BF16), bexp,
                       a_v_norm[j].reshape(1, D), a_w_out[j].astype(BF16))
        else:
            qn = jnp.tile(b_q_norm[j], HEADS).reshape(1, D)
            kn = jnp.tile(b_k_norm[j], HEADS).reshape(1, D)
            q, k, v, kf, vf = _qkv(xt, mod_l, n1, b_w_qkv[j].astype(BF16), qn, kn, ind, indt)
            new_k.append(kf[:SEG].reshape(n_prompt, SEQ, HEADS, HEAD_DIM))
            new_v.append(vf[:SEG].reshape(n_prompt, SEQ, HEADS, HEAD_DIM))
            w_out = b_w_out[j].astype(BF16)
            x_ctx = _ctx_attn(q, k, v, xt.reshape(T, D), mod_l, w_out)
            kc = cache_k[:, j].reshape(n_sample, SEQ, D).astype(BF16)
            vc = cache_v[:, j].reshape(n_sample, SEQ, D).astype(BF16)
            x = _na_attn(q, k, v, kc, vc, _na_bias_table(b_rpb[j]), x_ctx, mod_l, w_out)
        ht, xt, idx, g = _peer_select(x, mod_l, norm2_w[i].reshape(1, D), peer_w_q[i].T.astype(BF16),
                                      peer_sub_keys[i].astype(BF16))
        xt = _peer_expert(idx, ht, g, xt, mod_l[:, 5].reshape(8, 8, 128), pair_expand,
                          _pack_expert_table(peer_u, peer_v, i))

    x = xt.reshape(T, D)
    y_prompt = x[:SEG].reshape(x_prompt.shape)
    y_sample = x[SEG:].reshape(x_sample.shape)
    return (y_prompt, y_sample, jnp.stack(new_k, axis=1), jnp.stack(new_v, axis=1))
```

```python
import functools

import jax
import jax.numpy as jnp
import numpy as np
from jax import lax
from jax.experimental import pallas as pl
from jax.experimental.pallas import tpu as pltpu

F32 = jnp.float32
BF16 = jnp.bfloat16

D = 1024
DEPTH = 4
SEG = 4096
N_SEG = 3
T = N_SEG * SEG
SEQ = 256
GRID_W = 64
CHUNK = 128
A_GROUPS = 8
HEADS = 16
HEAD_DIM = 64
NA_ROWS = 8
NA_COLS = 16
P_HEADS = 8
P_HALF = 128
P_KEYS = 128
P_TOPK = 16
P_PAIRS = P_HEADS * P_TOPK
N_EXPERTS = P_KEYS * P_KEYS
EPS = 1e-6
NEG = -1e30

VMEM_LIMIT = 56 * 1024 * 1024

TM = 256
PEER_TB = 64
PEER_G = 8
PEER_ROWS = PEER_G * P_PAIRS


def _params(*sem):
    return pltpu.CompilerParams(dimension_semantics=sem, vmem_limit_bytes=VMEM_LIMIT)


def _dot(a, b):
    return jnp.dot(a, b, preferred_element_type=F32)


def _dot_nt(a, b):
    return lax.dot_general(a, b, (((1,), (1,)), ((), ())), preferred_element_type=F32)


def _split(a):
    hi = a.astype(BF16)
    lo = (a - hi.astype(F32)).astype(BF16)
    return hi, lo


def _dot_f32(a, b):
    ah, al = _split(a)
    bh, bl = _split(b)
    return _dot(ah, bh) + _dot(ah, bl) + _dot(al, bh)


def _dot_f32_exact_rhs(a, b_bf16):
    ah, al = _split(a)
    return _dot(ah, b_bf16) + _dot(al, b_bf16)


def _rms_mod(x, norm_w, scale, shift):
    y = x * lax.rsqrt(jnp.mean(x * x, axis=-1, keepdims=True) + EPS)
    return (y * norm_w) * (1.0 + scale) + shift


def _rows_from_tiles(ref, n):
    return jnp.concatenate([ref[pl.ds(c, n, stride=8), :] for c in range(D // 128)], axis=1)


def _tile_block(n, index_map):
    return pl.BlockSpec((n * 8, 128), index_map)


def _ada_kernel(cond_ref, w_ref, b_ref, o_ref):
    c = cond_ref[...]
    s = c * jax.nn.sigmoid(c)
    o_ref[...] = _dot_f32(s, w_ref[...]) + b_ref[...]


def _ada_mod(cond8, ada_w, ada_b):
    tn = 1024
    n = 6 * D
    out = pl.pallas_call(
        _ada_kernel,
        grid=(DEPTH, n // tn),
        in_specs=[
            pl.BlockSpec((8, D), lambda l, j: (0, 0)),
            pl.BlockSpec((None, D, tn), lambda l, j: (l, 0, j)),
            pl.BlockSpec((None, 1, tn), lambda l, j: (l, 0, j)),
        ],
        out_specs=pl.BlockSpec((None, 8, tn), lambda l, j: (l, 0, j)),
        out_shape=jax.ShapeDtypeStruct((DEPTH, 8, n), F32),
        compiler_params=_params("parallel", "parallel"),
        name="ada_mod",
    )(cond8, ada_w, ada_b.reshape(DEPTH, 1, n))
    return out.reshape(DEPTH, 8, 6, D)


def _mix_a_kernel(x_ref, mod_ref, n1_ref, win_ref, ws_ref, bexp_ref, vn_ref, wout_ref, o_ref):
    x = _rows_from_tiles(x_ref, TM)
    h = _rms_mod(x, n1_ref[...], mod_ref[1:2, :], mod_ref[0:1, :])
    z = jax.nn.gelu(_dot(h.astype(BF16), win_ref[...]))
    u = z[:, :D]
    v = z[:, D:]
    v = v * lax.rsqrt(jnp.mean(v * v, axis=-1, keepdims=True) + EPS) * vn_ref[...]
    vb = v.astype(BF16)
    rows = []
    for c in range(TM // CHUNK):
        cols = []
        for g in range(A_GROUPS):
            vg = vb[c * CHUNK:(c + 1) * CHUNK, g * 128:(g + 1) * 128]
            cols.append(_dot(ws_ref[g], vg))
        rows.append(jnp.concatenate(cols, axis=1) + bexp_ref[...])
    sv = jnp.concatenate(rows, axis=0)
    mix = _dot((u * sv).astype(BF16), wout_ref[...])
    o_ref[...] = x + mod_ref[2:3, :] * mix


def _mix_a(x, mod_l, n1, w_in, w_s, bexp, vn, w_out):
    const2 = lambda i: (0, 0)
    return pl.pallas_call(
        _mix_a_kernel,
        grid=(T // TM,),
        in_specs=[
            _tile_block(TM, lambda i: (i, 0)),
            pl.BlockSpec((None, 6, D), lambda i: ((i * TM) // SEG, 0, 0)),
            pl.BlockSpec((1, D), const2),
            pl.BlockSpec((D, 2 * D), const2),
            pl.BlockSpec((A_GROUPS, CHUNK, CHUNK), lambda i: (0, 0, 0)),
            pl.BlockSpec((CHUNK, D), const2),
            pl.BlockSpec((1, D), const2),
            pl.BlockSpec((D, D), const2),
        ],
        out_specs=pl.BlockSpec((TM, D), lambda i: (i, 0)),
        out_shape=jax.ShapeDtypeStruct((T, D), F32),
        compiler_params=_params("parallel"),
        name="mix_a",
    )(x, mod_l, n1, w_in, w_s, bexp, vn, w_out)


def _qkv_kernel(x_ref, mod_ref, n1_ref, w_ref, qn_ref, kn_ref, ind_ref, indt_ref,
                q_o, k_o, v_o, kf_o, vf_o):
    x = _rows_from_tiles(x_ref, TM)
    h = _rms_mod(x, n1_ref[...], mod_ref[1:2, :], mod_ref[0:1, :])
    qkv =_dot(h.astype(BF16), w_ref[...])

    def head_norm(t, w):
        ss = _dot_f32_exact_rhs(t * t, ind_ref[...])
        inv = lax.rsqrt(ss * (1.0 / HEAD_DIM) + EPS)
        return t * _dot_f32_exact_rhs(inv, indt_ref[...]) * w

    q = head_norm(qkv[:, :D], qn_ref[...])
    k = head_norm(qkv[:, D:2 * D], kn_ref[...])
    v = qkv[:, 2 * D:]
    q_o[...] = (q * (HEAD_DIM ** -0.5)).astype(BF16)
    k_o[...] = k.astype(BF16)
    v_o[...] = v.astype(BF16)
    kf_o[...] = k
    vf_o[...] = v


def _qkv(x, mod_l, n1, w_qkv, qn, kn, ind, indt):
    const2 = lambda i: (0, 0)
    blk = pl.BlockSpec((TM, D), lambda i: (i, 0))
    return pl.pallas_call(
        _qkv_kernel,
        grid=(T // TM,),
        in_specs=[
            _tile_block(TM, lambda i: (i, 0)),
            pl.BlockSpec((None, 6, D), lambda i: ((i * TM) // SEG, 0, 0)),
            pl.BlockSpec((1, D), const2),
            pl.BlockSpec((D, 3 * D), const2),
            pl.BlockSpec((1, D), const2),
            pl.BlockSpec((1, D), const2),
            pl.BlockSpec((D, 128), const2),
            pl.BlockSpec((128, D), const2),
        ],
        out_specs=[blk, blk, blk, blk, blk],
        out_shape=[jax.ShapeDtypeStruct((T, D), BF16)] * 3 + [jax.ShapeDtypeStruct((T, D), F32)] * 2,
        compiler_params=_params("parallel"),
        name="qkv",
    )(x, mod_l, n1, w_qkv, qn, kn, ind, indt)


def _head_pair_attention(q2, score_parts):
    lane = lax.broadcasted_iota(jnp.int32, (1, 128), 1)
    lo = lane < HEAD_DIM
    out = None
    for sub in range(2):
        keep = lo if sub == 0 else jnp.logical_not(lo)
        qm = jnp.where(keep, q2, jnp.zeros_like(q2))
        scores = []
        for k2, _, bias_fn in score_parts:
            s = _dot_nt(qm, k2)
            b = bias_fn(sub)
            scores.append(s if b is None else s + b)
        m = scores[0].max(axis=-1, keepdims=True)
        for s in scores[1:]:
            m = jnp.maximum(m, s.max(axis=-1, keepdims=True))
        den = None
        acc = None
        for s, (_, v2, _) in zip(scores, score_parts):
            p = jnp.exp(s - m)
            d = p.sum(axis=-1, keepdims=True)
            o = _dot(p.astype(BF16), v2)
            den = d if den is None else den + d
            acc = o if acc is None else acc + o
        o = acc / den
        out = o if sub == 0 else jnp.where(lo, out, o)
    return out


def _ctx_attn_kernel(q_ref, k_ref, v_ref, x_ref, mod_ref, wout_ref, o_ref):
    outs = []
    for hp in range(HEADS // 2):
        sl = slice(hp * 128, (hp + 1) * 128)
        outs.append(_head_pair_attention(
            q_ref[:, sl], [(k_ref[:, sl], v_ref[:, sl], lambda sub: None)]))
    o = jnp.concatenate(outs, axis=1).astype(BF16)
    o_ref[...] = x_ref[...] + mod_ref[2:3, :] * _dot(o, wout_ref[...])


def _ctx_attn(q, k, v, x, mod_l, w_out):
    n = SEG // SEQ
    blk = pl.BlockSpec((SEQ, D), lambda b: (b, 0))
    return pl.pallas_call(
        _ctx_attn_kernel,
        grid=(n,),
        in_specs=[blk, blk, blk, blk,
                  pl.BlockSpec((None, 6, D), lambda b: (0, 0, 0)),
                  pl.BlockSpec((D, D), lambda b: (0, 0))],
        out_specs=blk,
        out_shape=jax.ShapeDtypeStruct((T, D), F32),
        input_output_aliases={3: 0},
        compiler_params=_params("parallel"),
        name="ctx_attn",
    )(q, k, v, x, mod_l, w_out)


def _na_window_start(r):
    rows = SEG // GRID_W
    return jnp.clip(r - NA_ROWS // 2, 0, rows - NA_ROWS)


def _na_attn_kernel(q_ref, k_ref, v_ref, kc_ref, vc_ref, bias_ref, x_ref, mod_ref, wout_ref, o_ref):
    r = pl.program_id(1)
    start = pl.multiple_of(_na_window_start(r) * GRID_W, GRID_W)
    n_loc = NA_ROWS * GRID_W
    outs = []
    for hp in range(HEADS // 2):
        sl = slice(hp * 128, (hp + 1) * 128)
        kw = k_ref[pl.ds(start, n_loc), sl]
        vw = v_ref[pl.ds(start, n_loc), sl]
        outs.append(_head_pair_attention(
            q_ref[:, sl],
            [(kw, vw, lambda sub, hp=hp: bias_ref[2 * hp + sub]),
             (kc_ref[:, sl], vc_ref[:, sl], lambda sub: None)]))
    o = jnp.concatenate(outs, axis=1).astype(BF16)
    o_ref[...] = x_ref[...] + mod_ref[2:3, :] * _dot(o, wout_ref[...])


def _na_attn(q, k, v, kc, vc, bias, x, mod_l, w_out):
    rows = SEG // GRID_W
    nb = (T - SEG) // SEG
    qblk = pl.BlockSpec((GRID_W, D), lambda b, r: (rows + b * rows + r, 0))
    img = pl.BlockSpec((SEG, D), lambda b, r: (1 + b, 0))
    ctx = pl.BlockSpec((None, SEQ, D), lambda b, r: (b, 0, 0))

    def bias_map(b, r):
        return (_na_window_start(r) - r + NA_ROWS - 1, 0, 0, 0)

    return pl.pallas_call(
        _na_attn_kernel,
        grid=(nb, rows),
        in_specs=[qblk, img, img, ctx, ctx,
                  pl.BlockSpec((None, HEADS, GRID_W, NA_ROWS * GRID_W), bias_map),
                  qblk,
                  pl.BlockSpec((None, 6, D), lambda b, r: (1 + b, 0, 0)),
                  pl.BlockSpec((D, D), lambda b, r: (0, 0))],
        out_specs=qblk,
        out_shape=jax.ShapeDtypeStruct((T, D), F32),
        input_output_aliases={6: 0},
        compiler_params=_params("parallel", "arbitrary"),
        name="na_attn",
    )(q, k, v, kc, vc, bias, x, mod_l, w_out)


def _na_bias_table(rpb):
    qc = np.arange(GRID_W)
    c0 = np.clip(qc - NA_COLS // 2, 0, GRID_W - NA_COLS)
    kc = np.arange(GRID_W)
    inside = (kc[None, :] >= c0[:, None]) & (kc[None, :] < c0[:, None] + NA_COLS)
    dc = kc[None, :] - qc[:, None] + NA_COLS - 1
    cols = jnp.zeros(rpb.shape[:2] + (GRID_W, GRID_W), F32)
    for d in range(2 * NA_COLS - 1):
        cols = jnp.where(jnp.asarray(inside & (dc == d))[None, None], rpb[:, :, d, None, None], cols)
    cols = jnp.where(jnp.asarray(inside)[None, None], cols, NEG)
    tab = jnp.stack([cols[:, var:var + NA_ROWS] for var in range(NA_ROWS)], axis=0)
    tab = jnp.transpose(tab, (0, 1, 3, 2, 4))
    return tab.reshape(NA_ROWS, HEADS, GRID_W, NA_ROWS * GRID_W)


def _peer_select_kernel(x_ref, mod_ref, n2_ref, wqt_ref, sk_ref, h_o, xt_o, idx_o, g_o,
                        q_scr, idx_scr, g_scr):
    x = x_ref[...]
    h = _rms_mod(x, n2_ref[...], mod_ref[4:5, :], mod_ref[3:4, :])
    for c in range(D // 128):
        h_o[pl.ds(c, TM, stride=8), :] = h[:, c * 128:(c + 1) * 128]
        xt_o[pl.ds(c, TM, stride=8), :] = x[:, c * 128:(c + 1) * 128]
    q_scr[...] = _dot_nt(wqt_ref[...], h.astype(BF16))

    n_c = P_TOPK * P_TOPK
    key_pos = lax.broadcasted_iota(jnp.int32, (P_KEYS, TM), 0).astype(F32)
    rank = lax.broadcasted_iota(jnp.int32, (P_TOPK, TM), 0)
    rank8 = lax.broadcasted_iota(jnp.int32, (8, TM), 0)
    cand_pos = jnp.concatenate(
        [rank] + [a * P_TOPK + rank8 for a in range(1, 8)] + [(8 + rank8) * P_TOPK], axis=0).astype(F32)
    neg_inf = jnp.float32(-jnp.inf)

    def head(hd, carry):
        top_val, top_idx = [], []
        for p in range(2):
            off = pl.multiple_of(hd * 2 * P_HALF + p * P_HALF, P_HALF)
            s = _dot(sk_ref[p], q_scr[pl.ds(off, P_HALF), :].astype(BF16))
            vals, idxs = [], []
            for it in range(P_TOPK):
                m = jnp.max(s, axis=0, keepdims=True)
                am = jnp.min(jnp.where(s == m, key_pos, float(P_KEYS)), axis=0, keepdims=True)
                s = jnp.where(key_pos == am, neg_inf, s)
                vals.append(m)
                idxs.append(am)
            top_val.append(vals)
            top_idx.append(idxs)
        s2 = jnp.zeros((P_TOPK, TM), F32)
        i2 = jnp.zeros((P_TOPK, TM), F32)
        for it in range(P_TOPK):
            s2 = jnp.where(rank == it, top_val[1][it], s2)
            i2 = jnp.where(rank == it, top_idx[1][it], i2)
        s1_hi = jnp.zeros((8, TM), F32)
        i1_hi = jnp.zeros((8, TM), F32)
        for it in range(8, P_TOPK):
            s1_hi = jnp.where(rank8 == it - 8, top_val[0][it], s1_hi)
            i1_hi = jnp.where(rank8 == it - 8, top_idx[0][it], i1_hi)
        cs_parts = [top_val[0][0] + s2]
        ce_parts = [top_idx[0][0] * float(P_KEYS) + i2]
        for a in range(1, 8):
            live = rank8 < P_TOPK // (a + 1)
            cs_parts.append(jnp.where(live, top_val[0][a] + s2[:8], neg_inf))
            ce_parts.append(top_idx[0][a] * float(P_KEYS) + i2[:8])
        cs_parts.append(s1_hi + top_val[1][0])
        ce_parts.append(i1_hi * float(P_KEYS) + top_idx[1][0])
        cs = jnp.concatenate(cs_parts, axis=0)
        ce = jnp.concatenate(ce_parts, axis=0)
        sc = jnp.zeros((P_TOPK, TM), F32)
        ex_id = jnp.zeros((P_TOPK, TM), F32)
        best = None
        for it in range(P_TOPK):
            m = jnp.max(cs, axis=0, keepdims=True)
            am = jnp.min(jnp.where(cs == m, cand_pos, float(n_c)), axis=0, keepdims=True)
            hit = cand_pos == am
            e = jnp.max(jnp.where(hit, ce, -1.0), axis=0, keepdims=True)
            cs = jnp.where(hit, neg_inf, cs)
            sc = jnp.where(rank == it, m, sc)
            ex_id = jnp.where(rank == it, e, ex_id)
            if it == 0:
                best = m
        ex = jnp.exp(sc - best)
        rows = pl.ds(pl.multiple_of(hd * P_TOPK, P_TOPK), P_TOPK)
        idx_scr[rows, :] = ex_id
        g_scr[rows, :] = ex / jnp.sum(ex, axis=0, keepdims=True)
        return carry

    lax.fori_loop(0, P_HEADS, head, 0)
    idx_o[...] = idx_scr[...].T.astype(jnp.int32)
    g_o[...] = g_scr[...].T


def _peer_select(x, mod_l, n2, wqt, sk):
    const2 = lambda i: (0, 0)
    blk = pl.BlockSpec((TM, D), lambda i: (i, 0))
    tiles = _tile_block(TM, lambda i: (i, 0))
    tblk = pl.BlockSpec((TM, P_PAIRS), lambda i: (i, 0))
    return pl.pallas_call(
        _peer_select_kernel,
        grid=(T // TM,),
        in_specs=[
            blk,
            pl.BlockSpec((None, 6, D), lambda i: ((i * TM) // SEG, 0, 0)),
            pl.BlockSpec((1, D), const2),
            pl.BlockSpec((P_HEADS * 2 * P_HALF, D), const2),
            pl.BlockSpec((2, P_KEYS, P_HALF), lambda i: (0, 0, 0)),
        ],
        out_specs=[tiles, tiles, tblk, tblk],
        out_shape=[jax.ShapeDtypeStruct((T * 8, 128), F32),
                   jax.ShapeDtypeStruct((T * 8, 128), F32),
                   jax.ShapeDtypeStruct((T, P_PAIRS), jnp.int32),
                   jax.ShapeDtypeStruct((T, P_PAIRS), F32)],
        scratch_shapes=[pltpu.VMEM((P_HEADS * 2 * P_HALF, TM), F32),
                        pltpu.VMEM((P_PAIRS, TM), F32),
                        pltpu.VMEM((P_PAIRS, TM), F32)],
        compiler_params=_params("parallel"),
        name="peer_select",
    )(x, mod_l, n2, wqt, sk)


PEER_LANES = 2 * P_HEADS * P_PAIRS
PEER_EARLY = 96
PEER_EARLY_BY_TOKEN = (112, 112, 96, 96, 96, 96, 80, 80)
assert sum(PEER_EARLY_BY_TOKEN) == PEER_EARLY * PEER_G


def _peer_expert_kernel(idx_ref, h_ref, g_ref, x_ref, gate_ref, e2_ref, tab_ref, o_ref,
                        buf0, buf1, grep, sem):
    step = pl.program_id(0)
    n_batches = PEER_TB // PEER_G
    bufs = (buf0, buf1)

    def gather_wait(slot, j):
        rows = pl.ds(j * P_PAIRS, P_PAIRS)
        pltpu.make_async_copy(tab_ref.at[rows], bufs[slot].at[rows], sem.at[slot, j]).wait()

    def gather_start(tok, j, slot, k0=0, k1=P_PAIRS):
        for k in range(k0, k1):
            pltpu.make_async_copy(
                tab_ref.at[idx_ref[tok, k]],
                bufs[slot].at[j * P_PAIRS + k],
                sem.at[slot, j]).start(priority=k % 2)

    @pl.when(step == 0)
    def _():
        for j in range(PEER_G):
            gather_start(j, j, 0)

    grep[...] = _dot_f32_exact_rhs(g_ref[...], e2_ref[...])

    sub = lax.broadcasted_iota(jnp.int32, (8, PEER_LANES), 0)
    lane = lax.broadcasted_iota(jnp.int32, (8, PEER_LANES), 1)
    is_u = (lane % 16) == sub
    is_v = (lane % 16) == sub + 8

    def token_rows(slot, j):
        tiles = bufs[slot][j * P_PAIRS:(j + 1) * P_PAIRS]
        return pltpu.bitcast(tiles.reshape(P_PAIRS * 8, 128), BF16)

    def tile_rows(b, j):
        return pl.ds(pl.multiple_of((b * PEER_G + j) * 8, 8), 8)

    def combine(b, slot, j, w):
        lj = jnp.where(is_v, jnp.broadcast_to(w[j:j + 1, :], (8, PEER_LANES)), 0.0)
        y = _dot(lj.astype(BF16), token_rows(slot, j))
        o_ref[tile_rows(b, j), :] = x_ref[tile_rows(b, j), :] + gate_ref[...] * y

    def batch(b, slot):
        part = jnp.zeros((PEER_G, PEER_LANES), F32)
        nxt = (b + 1) * PEER_G
        for j0 in range(0, PEER_G, 2):
            for j in (j0, j0 + 1):
                gather_start(nxt + j, j, 1 - slot, 0, PEER_EARLY_BY_TOKEN[j])
            for j in (j0, j0 + 1):
                gather_wait(slot, j)
            for j in (j0, j0 + 1):
                r = _dot_nt(h_ref[tile_rows(b, j), :].astype(BF16), token_rows(slot, j))
                c = jnp.sum(jnp.where(is_u, r, 0.0), axis=0, keepdims=True)
                part = jnp.where(sub == j, c, part)
        a = part
        for k in (1, 2, 4, 8):
            up = pltpu.roll(a, k, 1)
            down = pltpu.roll(a, PEER_LANES - k, 1)
            a = a + jnp.where((lane & k) != 0, up, down)
        w = jax.nn.gelu(a) * grep[pl.ds(pl.multiple_of(b * PEER_G, PEER_G), PEER_G), :]
        for j in range(PEER_G):
            gather_start(nxt + j, j, 1 - slot, PEER_EARLY_BY_TOKEN[j], P_PAIRS)
            combine(b, slot, j, w)

    def batch_pair(i, carry):
        batch(2 * i, 0)
        batch(2 * i + 1, 1)
        return carry

    lax.fori_loop(0, n_batches // 2, batch_pair, 0)

    @pl.when(step == pl.num_programs(0) - 1)
    def _():
        for j in range(PEER_G):
            gather_wait(0, j)


def _peer_expert(idx, h_tiles, g, x_tiles, gate_tiles, e2, table):
    n_steps = T // PEER_TB
    idx = idx.reshape(n_steps, PEER_TB, P_PAIRS)
    idx = jnp.concatenate([idx, jnp.roll(idx[:, :PEER_G], -1, axis=0)], axis=1)
    blk = pl.BlockSpec((PEER_TB * 8, 128), lambda i: (i, 0))
    return pl.pallas_call(
        _peer_expert_kernel,
        grid=(n_steps,),
        in_specs=[
            pl.BlockSpec((None, PEER_TB + PEER_G, P_PAIRS), lambda i: (i, 0, 0), memory_space=pltpu.SMEM),
            blk,
            pl.BlockSpec((PEER_TB, P_PAIRS), lambda i: (i, 0)),
            blk,
            pl.BlockSpec((None, 8, 128), lambda i: ((i * PEER_TB) // SEG, 0, 0)),
            pl.BlockSpec((P_PAIRS, PEER_LANES), lambda i: (0, 0)),
            pl.BlockSpec(memory_space=pl.ANY),
        ],
        out_specs=blk,
        out_shape=jax.ShapeDtypeStruct((T * 8, 128), F32),
        scratch_shapes=[pltpu.VMEM((PEER_ROWS, 8, 128), jnp.uint32),
                        pltpu.VMEM((PEER_ROWS, 8, 128), jnp.uint32),
                        pltpu.VMEM((PEER_TB, PEER_LANES), F32),
                        pltpu.SemaphoreType.DMA((2, PEER_G))],
        compiler_params=_params("arbitrary"),
        name="peer_expert",
    )(idx, h_tiles, g, x_tiles, gate_tiles, e2, table)


PACK_ROWS = 512


def _pack_kernel(u_ref, v_ref, o_ref):
    def bf16_bits(t):
        return pltpu.bitcast(t.astype(BF16).astype(F32), jnp.uint32)

    for half, ref in enumerate((u_ref, v_ref)):
        for s in range(4):
            lo = bf16_bits(ref[:, (2 * s) * 128:(2 * s + 1) * 128])
            hi = bf16_bits(ref[:, (2 * s + 1) * 128:(2 * s + 2) * 128])
            o_ref[pl.ds(4 * half + s, PACK_ROWS, stride=8), :] = (lo >> 16) | hi


def _pack_expert_table(u_all, v_all, layer):
    blk = pl.BlockSpec((None, PACK_ROWS, D), lambda i: (layer, i, 0))
    out = pl.pallas_call(
        _pack_kernel,
        grid=(N_EXPERTS // PACK_ROWS,),
        in_specs=[blk, blk],
        out_specs=pl.BlockSpec((PACK_ROWS * 8, 128), lambda i: (i, 0)),
        out_shape=jax.ShapeDtypeStruct((N_EXPERTS * 8, 128), jnp.uint32),
        compiler_params=_params("parallel"),
        name="pack_table",
    )(u_all, v_all)
    return out.reshape(N_EXPERTS, 8, 128)


def kernel(x_prompt, x_sample, cache_k, cache_v, c, c_ctx, ada_w, ada_b, norm1_w, norm2_w,
           a_w_in, a_w_s, a_b_s, a_v_norm, a_w_out, b_w_qkv, b_q_norm, b_k_norm, b_rpb,
           b_w_out, peer_w_q, peer_sub_keys, peer_u, peer_v):
    n_prompt = x_prompt.shape[0]
    n_sample = x_sample.shape[0]
    xt = jnp.concatenate([x_prompt.reshape(-1, D), x_sample.reshape(-1, D)], axis=0).reshape(T * 8, 128)

    cond8 = jnp.zeros((8, D), F32).at[0].set(c_ctx).at[1:1 + n_sample].set(c)
    mod = _ada_mod(cond8, ada_w, ada_b)

    head_of = jnp.arange(D) // HEAD_DIM
    ind = (head_of[:, None] == jnp.arange(128)[None, :]).astype(BF16)
    indt = ind.T
    pair_expand = (jnp.arange(P_PAIRS)[:, None] == jnp.arange(PEER_LANES)[None, :] // 16).astype(BF16)

    new_k, new_v = [], []
    for i in range(DEPTH):
        j = i // 2
        mod_l = mod[i]
        n1 = norm1_w[i].reshape(1, D)
        if i % 2 == 0:
            bexp = jnp.repeat(a_b_s[j].T, D // A_GROUPS, axis=1)
            x = _mix_a(xt, mod_l, n1, a_w_in[j].astype(BF16), a_w_s[j].astype(BF16), bexp,
                       a_v_norm[j].reshape(1, D), a_w_out[j].astype(BF16))
        else:
            qn = jnp.tile(b_q_norm[j], HEADS).reshape(1, D)
            kn = jnp.tile(b_k_norm[j], HEADS).reshape(1, D)
            q, k, v, kf, vf = _qkv(xt, mod_l, n1, b_w_qkv[j].astype(BF16), qn, kn, ind, indt)
            new_k.append(kf[:SEG].reshape(n_prompt, SEQ, HEADS, HEAD_DIM))
            new_v.append(vf[:SEG].reshape(n_prompt, SEQ, HEADS, HEAD_DIM))
            w_out = b_w_out[j].astype(BF16)
            x_ctx = _ctx_attn(q, k, v, xt.reshape(T, D), mod_l, w_out)
            kc = cache_k[:, j].reshape(n_sample, SEQ, D).astype(BF16)
            vc = cache_v[:, j].reshape(n_sample, SEQ, D).astype(BF16)
            x = _na_attn(q, k, v, kc, vc, _na_bias_table(b_rpb[j]), x_ctx, mod_l, w_out)
        ht, xt, idx, g = _peer_select(x, mod_l, norm2_w[i].reshape(1, D), peer_w_q[i].T.astype(BF16),
                                      peer_sub_keys[i].astype(BF16))
        xt = _peer_expert(idx, ht, g, xt, mod_l[:, 5].reshape(8, 8, 128), pair_expand,
                          _pack_expert_table(peer_u, peer_v, i))

    x = xt.reshape(T, D)
    y_prompt = x[:SEG].reshape(x_prompt.shape)
    y_sample = x[SEG:].reshape(x_sample.shape)
    return (y_prompt, y_sample, jnp.stack(new_k, axis=1), jnp.stack(new_v, axis=1))
```
